```python
import math
import jax, jax.numpy as jnp
from jax import lax
import numpy as np

D_MODEL = 2048
BATCH = 2
SEQ = 4096
DEPTH = 1

GLA_HEADS = 4
GLA_DK = (D_MODEL // 2) // GLA_HEADS
GLA_DV = D_MODEL // GLA_HEADS
GLA_K_WIDTH = GLA_HEADS * GLA_DK
GLA_V_WIDTH = GLA_HEADS * GLA_DV
GLA_GATE_RANK = 16
GLA_GATE_NORMALIZER = 16.0
GLA_CHUNK = 64

DIFF_HEAD_DIM = 128
DIFF_HEADS = D_MODEL // (2 * DIFF_HEAD_DIM)
DIFF_V_DIM = 2 * DIFF_HEAD_DIM
DIFF_QK_WIDTH = DIFF_HEADS * 2 * DIFF_HEAD_DIM
DIFF_V_WIDTH = DIFF_HEADS * DIFF_V_DIM
ATTN_BLOCK = 128
ROPE_THETA = 500000.0
ROPE_DIM = DIFF_HEAD_DIM // 4

FFN_HIDDEN = 5632
CONV_WIDTH = 3
EPS = 1e-6
SUBLN_EPS = 1e-5

IN_SPLITS = [GLA_K_WIDTH, GLA_K_WIDTH, GLA_V_WIDTH, GLA_GATE_RANK, GLA_V_WIDTH,
             DIFF_QK_WIDTH, DIFF_QK_WIDTH, DIFF_V_WIDTH, D_MODEL, D_MODEL]
IN_WIDTH = int(sum(IN_SPLITS))
IN_SPLIT_POINTS = [int(i) for i in np.cumsum(IN_SPLITS)[:-1]]

kernel_name = "hybrid_gla_diffattn_convffn"


def rms_norm(x, g, eps=EPS):
    xf = x.astype(jnp.float32)
    y = xf * lax.rsqrt(jnp.mean(xf * xf, axis=-1, keepdims=True) + eps)
    return (y * g.astype(jnp.float32)).astype(x.dtype)


def lambda_init_fn(layer_idx):
    return 0.8 - 0.6 * math.exp(-0.3 * layer_idx)


def rope_partial(x, positions):
    inv_freq = ROPE_THETA ** (-jnp.arange(0, ROPE_DIM, 2, dtype=jnp.float32) / ROPE_DIM)
    ang = positions.astype(jnp.float32)[:, :, None] * inv_freq
    cos = jnp.cos(ang)[:, :, None, :]
    sin = jnp.sin(ang)[:, :, None, :]
    xr = x[..., :ROPE_DIM].astype(jnp.float32)
    x1, x2 = xr[..., :ROPE_DIM // 2], xr[..., ROPE_DIM // 2:]
    rot = jnp.concatenate([x1 * cos - x2 * sin, x2 * cos + x1 * sin], axis=-1)
    return jnp.concatenate([rot.astype(x.dtype), x[..., ROPE_DIM:]], axis=-1)


def gla_chunked(q, k, v, log_decay):
    B, T, H, dk = q.shape
    dv = v.shape[-1]
    C = GLA_CHUNK
    N = T // C

    def to_chunks(a):
        return a.reshape(B, N, C, H, a.shape[-1]).transpose(0, 3, 1, 2, 4)

    qf = to_chunks(q.astype(jnp.float32) * (dk ** -0.5))
    kf = to_chunks(k.astype(jnp.float32))
    vf = to_chunks(v.astype(jnp.float32))
    b = jnp.cumsum(to_chunks(log_decay.astype(jnp.float32)), axis=3)
    b_last = b[:, :, :, -1:, :]

    q_dec = qf * jnp.exp(b)
    k_inv = kf * jnp.exp(-b)
    k_tail = kf * jnp.exp(b_last - b)

    causal = jnp.tril(jnp.ones((C, C), dtype=bool))
    A = jnp.einsum('bhncd,bhnsd->bhncs', q_dec, k_inv)
    A = jnp.where(causal, A, 0.0)
    o_intra = jnp.einsum('bhncs,bhnse->bhnce', A, vf)

    chunk_kv = jnp.einsum('bhncd,bhnce->bhnde', k_tail, vf)
    chunk_decay = jnp.exp(b_last[:, :, :, 0, :])

    def step(S, inp):
        kv_n, dec_n = inp
        return S * dec_n[..., None] + kv_n, S

    S0 = jnp.zeros((B, H, dk, dv), jnp.float32)
    _, S_in = lax.scan(step, S0, (jnp.moveaxis(chunk_kv, 2, 0), jnp.moveaxis(chunk_decay, 2, 0)))
    S_in = jnp.moveaxis(S_in, 0, 2)
    o_inter = jnp.einsum('bhncd,bhnde->bhnce', q_dec, S_in)

    o = o_intra + o_inter
    return o.transpose(0, 2, 3, 1, 4).reshape(B, T, H, dv)


def diff_attention(q, k, v, lam):
    B, T, H2, d = q.shape
    H = H2 // 2
    NB = T // ATTN_BLOCK
    qh = jnp.transpose(q, (0, 2, 1, 3)).astype(jnp.float32) * (d ** -0.5)
    kh = jnp.transpose(k, (0, 2, 1, 3)).astype(jnp.float32)
    vh = jnp.transpose(v, (0, 2, 1, 3)).astype(jnp.float32)
    q_blocks = qh.reshape(B, H2, NB, ATTN_BLOCK, d).transpose(2, 0, 1, 3, 4)
    key_idx = jnp.arange(T)

    def attend(args):
        qb, blk = args
        s = jnp.einsum('bhqd,bhkd->bhqk', qb, kh)
        q_idx = blk * ATTN_BLOCK + jnp.arange(ATTN_BLOCK)
        s = jnp.where(key_idx[None, :] <= q_idx[:, None], s, -jnp.inf)
        p = jax.nn.softmax(s, axis=-1).reshape(B, H, 2, ATTN_BLOCK, T)
        a = p[:, :, 0] - lam * p[:, :, 1]
        return jnp.einsum('bhqk,bhke->bhqe', a, vh)

    o = lax.map(attend, (q_blocks, jnp.arange(NB)))
    return o.transpose(1, 0, 3, 2, 4).reshape(B, T, H, v.shape[-1])


def causal_depthwise_conv(a, w, bias):
    T = a.shape[1]
    a_pad = jnp.pad(a, ((0, 0), (CONV_WIDTH - 1, 0), (0, 0)))
    out = bias
    for j in range(CONV_WIDTH):
        out = out + a_pad[:, j:j + T, :] * w[j]
    return out


def setup_inputs(seed: int = 0) -> dict:
    key = jax.random.key(seed)
    ks = jax.random.split(key, 24)
    f32 = jnp.float32
    nrm = lambda k, shape, scale: jax.random.normal(k, shape, f32) * scale
    L = DEPTH
    return {
        "x": nrm(ks[0], (BATCH, SEQ, D_MODEL), 1.0),
        "positions": jnp.broadcast_to(jnp.arange(SEQ, dtype=jnp.int32), (BATCH, SEQ)),
        "norm1_g": 1.0 + nrm(ks[1], (L, D_MODEL), 0.02),
        "w_in": nrm(ks[2], (L, D_MODEL, IN_WIDTH), D_MODEL ** -0.5),
        "w_gk_up": nrm(ks[3], (L, GLA_GATE_RANK, GLA_K_WIDTH), GLA_GATE_RANK ** -0.5),
        "b_gk": nrm(ks[4], (L, GLA_K_WIDTH), 0.01),
        "gla_norm_g": 1.0 + nrm(ks[5], (L, GLA_DV), 0.02),
        "q_norm_g": 1.0 + nrm(ks[6], (L, DIFF_HEAD_DIM), 0.02),
        "k_norm_g": 1.0 + nrm(ks[7], (L, DIFF_HEAD_DIM), 0.02),
        "lambda_q1": nrm(ks[8], (L, DIFF_HEAD_DIM), 0.1),
        "lambda_k1": nrm(ks[9], (L, DIFF_HEAD_DIM), 0.1),
        "lambda_q2": nrm(ks[10], (L, DIFF_HEAD_DIM), 0.1),
        "lambda_k2": nrm(ks[11], (L, DIFF_HEAD_DIM), 0.1),
        "subln_g": 1.0 + nrm(ks[12], (L, DIFF_V_DIM), 0.02),
        "w_branch": nrm(ks[13], (L, GLA_V_WIDTH + DIFF_V_WIDTH, D_MODEL), GLA_V_WIDTH ** -0.5),
        "w_out": nrm(ks[14], (L, D_MODEL, D_MODEL), D_MODEL ** -0.5),
        "norm2_g": 1.0 + nrm(ks[15], (L, D_MODEL), 0.02),
        "w_ffn_up": nrm(ks[16], (L, D_MODEL, 2 * FFN_HIDDEN), D_MODEL ** -0.5),
        "conv_w": nrm(ks[17], (L, CONV_WIDTH, FFN_HIDDEN), CONV_WIDTH ** -0.5),
        "conv_b": nrm(ks[18], (L, FFN_HIDDEN), 0.01),
        "w_ffn_down": nrm(ks[19], (L, FFN_HIDDEN, D_MODEL), FFN_HIDDEN ** -0.5),
    }


def reference(x, positions, norm1_g, w_in, w_gk_up, b_gk, gla_norm_g, q_norm_g, k_norm_g,
              lambda_q1, lambda_k1, lambda_q2, lambda_k2, subln_g, w_branch, w_out,
              norm2_g, w_ffn_up, conv_w, conv_b, w_ffn_down):
    B, T, _ = x.shape
    for l in range(DEPTH):
        lambda_init = lambda_init_fn(l)
        h = rms_norm(x, norm1_g[l])
        u = h @ w_in[l]
        (g_q, g_k, g_v, g_lr, g_out, d_q, d_k, d_v, m_gla, m_diff) = jnp.split(u, IN_SPLIT_POINTS, axis=-1)

        gk = (g_lr @ w_gk_up[l] + b_gk[l]).astype(jnp.float32)
        log_decay = jax.nn.log_sigmoid(gk) / GLA_GATE_NORMALIZER
        o_gla = gla_chunked(g_q.reshape(B, T, GLA_HEADS, GLA_DK),
                            g_k.reshape(B, T, GLA_HEADS, GLA_DK),
                            g_v.reshape(B, T, GLA_HEADS, GLA_DV),
                            log_decay.reshape(B, T, GLA_HEADS, GLA_DK)).astype(x.dtype)
        o_gla = rms_norm(o_gla, gla_norm_g[l]) * jax.nn.silu(g_out.reshape(B, T, GLA_HEADS, GLA_DV))
        o_gla = o_gla.reshape(B, T, GLA_V_WIDTH)

        q = rms_norm(d_q.reshape(B, T, 2 * DIFF_HEADS, DIFF_HEAD_DIM), q_norm_g[l])
        k = rms_norm(d_k.reshape(B, T, 2 * DIFF_HEADS, DIFF_HEAD_DIM), k_norm_g[l])
        q = rope_partial(q, positions)
        k = rope_partial(k, positions)
        lam = (jnp.exp(jnp.sum(lambda_q1[l].astype(jnp.float32) * lambda_k1[l].astype(jnp.float32)))
               - jnp.exp(jnp.sum(lambda_q2[l].astype(jnp.float32) * lambda_k2[l].astype(jnp.float32)))
               + lambda_init)
        o_diff = diff_attention(q, k, d_v.reshape(B, T, DIFF_HEADS, DIFF_V_DIM), lam).astype(x.dtype)
        o_diff = rms_norm(o_diff, subln_g[l], SUBLN_EPS) * (1.0 - lambda_init)
        o_diff = o_diff.reshape(B, T, DIFF_V_WIDTH)

        y_gla = o_gla @ w_branch[l][:GLA_V_WIDTH]
        y_diff = o_diff @ w_branch[l][GLA_V_WIDTH:]
        mixed = jax.nn.sigmoid(m_gla) * y_gla + jax.nn.sigmoid(m_diff) * y_diff
        x = x + mixed @ w_out[l]

        h2 = rms_norm(x, norm2_g[l])
        up = h2 @ w_ffn_up[l]
        a, val = up[..., :FFN_HIDDEN], up[..., FFN_HIDDEN:]
        a = causal_depthwise_conv(a, conv_w[l], conv_b[l])
        x = x + (jax.nn.silu(a) * val) @ w_ffn_down[l]
    return x
```

```python
import functools
import math

import jax
import jax.numpy as jnp
from jax import lax
from jax.experimental import pallas as pl
from jax.experimental.pallas import tpu as pltpu

F32 = jnp.float32
BF16 = jnp.bfloat16

D_MODEL = 2048
GLA_HEADS = 4
GLA_DK = 256
GLA_DV = 512
GLA_K_WIDTH = GLA_HEADS * GLA_DK
GLA_V_WIDTH = GLA_HEADS * GLA_DV
GLA_GATE_RANK = 16
GLA_GATE_NORMALIZER = 16.0
GLA_CHUNK = 64
DIFF_HEAD_DIM = 128
DIFF_HEADS = 8
DIFF_V_DIM = 256
ROPE_THETA = 500000.0
ROPE_DIM = 32
FFN_HIDDEN = 5632
CONV_WIDTH = 3
EPS = 1e-6
SUBLN_EPS = 1e-5
LAMBDA_INIT = 0.8 - 0.6 * math.exp(-0.3 * 0)

LR_START = 2 * GLA_K_WIDTH + GLA_V_WIDTH
LR_END = LR_START + GLA_GATE_RANK
PACKED_WIDTH = 16384
COL_GQ, COL_GK, COL_GV, COL_GOUT = 0, 1024, 2048, 4096
COL_DQ, COL_DK, COL_DV, COL_MGLA, COL_MDIFF = 6144, 8192, 10240, 12288, 14336

LANES = 128
SUBLANES = 8
VMEM_LIMIT_BYTES = 56 * 1024 * 1024

NT_DIMS = (((1,), (1,)), ((), ()))
TN_DIMS = (((0,), (0,)), ((), ()))


def _params(semantics):
    return pltpu.CompilerParams(dimension_semantics=semantics, vmem_limit_bytes=VMEM_LIMIT_BYTES)


def _sigmoid(x):
    return 1.0 / (1.0 + jnp.exp(-x))


def _log_sigmoid(x):
    return jnp.minimum(x, 0.0) - jnp.log1p(jnp.exp(-jnp.abs(x)))


def _rms_scale(x, eps):
    return x * lax.rsqrt(jnp.mean(x * x, axis=-1, keepdims=True) + eps)


def _inproj_kernel(x_ref, g_ref, w_ref, wlr_ref, u_ref, glr_ref, h_scr):
    @pl.when(pl.program_id(1) == 0)
    def _():
        h = (_rms_scale(x_ref[...], EPS) * g_ref[...]).astype(BF16)
        h_scr[...] = h
        glr_ref[...] = jnp.dot(h, wlr_ref[...], preferred_element_type=F32)

    u_ref[...] = jnp.dot(h_scr[...], w_ref[...], preferred_element_type=F32).astype(BF16)


def _in_proj(x2d, g, w_packed, w_lr, *, tm, tn):
    m = x2d.shape[0]
    return pl.pallas_call(
        _inproj_kernel,
        out_shape=(jax.ShapeDtypeStruct((m, PACKED_WIDTH), BF16),
                   jax.ShapeDtypeStruct((m, LANES), F32)),
        grid=(m // tm, PACKED_WIDTH // tn),
        in_specs=[
            pl.BlockSpec((tm, D_MODEL), lambda i, j: (i, 0)),
            pl.BlockSpec((1, D_MODEL), lambda i, j: (0, 0)),
            pl.BlockSpec((D_MODEL, tn), lambda i, j: (0, j)),
            pl.BlockSpec((D_MODEL, LANES), lambda i, j: (0, 0)),
        ],
        out_specs=(pl.BlockSpec((tm, tn), lambda i, j: (i, j)),
                   pl.BlockSpec((tm, LANES), lambda i, j: (i, 0))),
        scratch_shapes=[pltpu.VMEM((tm, D_MODEL), BF16)],
        compiler_params=_params(("parallel", "arbitrary")),
        name="in_proj",
    )(x2d, g, w_packed, w_lr)


def _qkprep_kernel(u_ref, pos_ref, g_ref, freq_ref, sgn_ref, o_ref):
    scale = jnp.where(pl.program_id(0) == 0, DIFF_HEAD_DIM ** -0.5, 1.0).astype(F32)
    ang = pos_ref[...].astype(F32) * freq_ref[...]
    cosf = jnp.cos(ang)
    sinf = jnp.sin(ang) * sgn_ref[...]
    lane = lax.broadcasted_iota(jnp.int32, cosf.shape, 1)
    first_half = lane < ROPE_DIM // 2
    g = g_ref[0]
    for h in range(D_MODEL // DIFF_HEAD_DIM):
        cols = slice(h * DIFF_HEAD_DIM, (h + 1) * DIFF_HEAD_DIM)
        y = _rms_scale(u_ref[:, cols].astype(F32), EPS) * g
        partner = jnp.where(first_half,
                            pltpu.roll(y, DIFF_HEAD_DIM - ROPE_DIM // 2, 1),
                            pltpu.roll(y, ROPE_DIM // 2, 1))
        o_ref[0, :, cols] = ((y * cosf + partner * sinf) * scale).astype(BF16)


def _qk_prep(u, pos2d, qk_gain, freq, sgn, *, tm):
    m = u.shape[0]
    blk0 = COL_DQ // D_MODEL
    return pl.pallas_call(
        _qkprep_kernel,
        out_shape=jax.ShapeDtypeStruct((2, m, D_MODEL), BF16),
        grid=(2, m // tm),
        in_specs=[
            pl.BlockSpec((tm, D_MODEL), lambda w, i: (i, blk0 + w)),
            pl.BlockSpec((tm, 1), lambda w, i: (i, 0)),
            pl.BlockSpec((1, 1, DIFF_HEAD_DIM), lambda w, i: (w, 0, 0)),
            pl.BlockSpec((1, DIFF_HEAD_DIM), lambda w, i: (0, 0)),
            pl.BlockSpec((1, DIFF_HEAD_DIM), lambda w, i: (0, 0)),
        ],
        out_specs=pl.BlockSpec((1, tm, D_MODEL), lambda w, i: (w, i, 0)),
        compiler_params=_params(("parallel", "parallel")),
        name="qk_prep",
    )(u, pos2d, qk_gain, freq, sgn)


def _gla_kernel(q_ref, k_ref, v_ref, go_ref, glr_ref, wup_ref, bgk_ref, gn_ref, o_ref, s_scr, *, tb):
    @pl.when(pl.program_id(2) == 0)
    def _():
        s_scr[...] = jnp.zeros_like(s_scr)

    c = GLA_CHUNK
    gk = jnp.dot(glr_ref[...], wup_ref[...], preferred_element_type=F32,
                 precision=lax.Precision.HIGHEST) + bgk_ref[...]
    log_decay = _log_sigmoid(gk) / GLA_GATE_NORMALIZER
    row = lax.broadcasted_iota(jnp.int32, (tb, tb), 0)
    col = lax.broadcasted_iota(jnp.int32, (tb, tb), 1)
    shift = c.bit_length() - 1
    tri = jnp.where((jnp.right_shift(row, shift) == jnp.right_shift(col, shift)) & (row >= col),
                    1.0, 0.0).astype(F32)
    b_all = jnp.dot(tri, log_decay, preferred_element_type=F32, precision=lax.Precision.HIGHEST)

    causal = (lax.broadcasted_iota(jnp.int32, (c, c), 0) >= lax.broadcasted_iota(jnp.int32, (c, c), 1))
    gn = gn_ref[...]
    for n in range(tb // c):
        rows = slice(n * c, (n + 1) * c)
        b = b_all[rows]
        b_last = b[c - 1:c]
        q = q_ref[rows, :].astype(F32) * (GLA_DK ** -0.5)
        k = k_ref[rows, :].astype(F32)
        v = v_ref[rows, :]
        q_dec = (q * jnp.exp(b)).astype(BF16)
        k_inv = (k * jnp.exp(-b)).astype(BF16)
        k_tail = (k * jnp.exp(b_last - b)).astype(BF16)
        a = lax.dot_general(q_dec, k_inv, NT_DIMS, preferred_element_type=F32)
        a = jnp.where(causal, a, 0.0).astype(BF16)
        s_in = s_scr[...]
        o = (jnp.dot(a, v, preferred_element_type=F32)
             + lax.dot_general(q_dec, s_in.astype(BF16), NT_DIMS, preferred_element_type=F32))
        kv_t = lax.dot_general(v, k_tail, TN_DIMS, preferred_element_type=F32)
        s_scr[...] = s_in * jnp.exp(b_last) + kv_t
        go = go_ref[rows, :].astype(F32)
        o = _rms_scale(o, EPS) * gn * (go * _sigmoid(go))
        o_ref[rows, :] = o.astype(BF16)


def _gla(u, glr, wup_pad, b_gk, gla_norm_g, *, batch, seq, tb):
    m = u.shape[0]
    nt = seq // tb
    kernel = functools.partial(_gla_kernel, tb=tb)
    return pl.pallas_call(
        kernel,
        out_shape=jax.ShapeDtypeStruct((m, GLA_V_WIDTH), BF16),
        grid=(batch, GLA_HEADS, nt),
        in_specs=[
            pl.BlockSpec((tb, GLA_DK), lambda b, h, t: (b * nt + t, COL_GQ // GLA_DK + h)),
            pl.BlockSpec((tb, GLA_DK), lambda b, h, t: (b * nt + t, COL_GK // GLA_DK + h)),
            pl.BlockSpec((tb, GLA_DV), lambda b, h, t: (b * nt + t, COL_GV // GLA_DV + h)),
            pl.BlockSpec((tb, GLA_DV), lambda b, h, t: (b * nt + t, COL_GOUT // GLA_DV + h)),
            pl.BlockSpec((tb, LANES), lambda b, h, t: (b * nt + t, 0)),
            pl.BlockSpec((LANES, GLA_DK), lambda b, h, t: (0, h)),
            pl.BlockSpec((1, GLA_DK), lambda b, h, t: (0, h)),
            pl.BlockSpec((1, GLA_DV), lambda b, h, t: (0, 0)),
        ],
        out_specs=pl.BlockSpec((tb, GLA_DV), lambda b, h, t: (b * nt + t, h)),
        scratch_shapes=[pltpu.VMEM((GLA_DV, GLA_DK), F32)],
        compiler_params=_params(("parallel", "parallel", "arbitrary")),
        name="gla",
    )(u, u, u, u, glr, wup_pad, b_gk, gla_norm_g)


def _attn_kernel(lq1_ref, lk1_ref, lq2_ref, lk2_ref, sg_ref, q_ref, k_ref, v_ref, o_ref,
                 m_scr, l_scr, acc_scr, *, tq):
    qi = pl.program_id(2)
    d = DIFF_HEAD_DIM
    q = q_ref[0]
    qs = (q[:, :d], q[:, d:])

    m_scr[...] = jnp.full_like(m_scr, -jnp.inf)
    l_scr[...] = jnp.zeros_like(l_scr)
    acc_scr[...] = jnp.zeros_like(acc_scr)

    def step(j, masked):
        start = pl.multiple_of(j * tq, tq)
        kblk = k_ref[0, pl.ds(start, tq), :]
        vblk = v_ref[pl.ds(start, tq), :]
        for sub in range(2):
            s = lax.dot_general(qs[sub], kblk[:, sub * d:(sub + 1) * d], NT_DIMS,
                                preferred_element_type=F32)
            if masked:
                keep = (lax.broadcasted_iota(jnp.int32, s.shape, 0)
                        >= lax.broadcasted_iota(jnp.int32, s.shape, 1))
                s = jnp.where(keep, s, -jnp.inf)
            m_old = m_scr[sub]
            m_new = jnp.maximum(m_old, jnp.max(s, axis=-1, keepdims=True))
            alpha = jnp.exp(m_old - m_new)
            p = jnp.exp(s - m_new)
            l_scr[sub] = alpha * l_scr[sub] + jnp.sum(p, axis=-1, keepdims=True)
            acc_scr[sub] = alpha * acc_scr[sub] + jnp.dot(p.astype(BF16), vblk,
                                                          preferred_element_type=F32)
            m_scr[sub] = m_new

    def body(j, carry):
        step(j, masked=False)
        return carry

    lax.fori_loop(0, qi, body, 0)
    step(qi, masked=True)

    lam = (jnp.exp(jnp.sum(lq1_ref[...] * lk1_ref[...], axis=-1, keepdims=True))
           - jnp.exp(jnp.sum(lq2_ref[...] * lk2_ref[...], axis=-1, keepdims=True))
           + LAMBDA_INIT)
    o = acc_scr[0] / l_scr[0] - lam * (acc_scr[1] / l_scr[1])
    o = _rms_scale(o, SUBLN_EPS) * sg_ref[...] * (1.0 - LAMBDA_INIT)
    o_ref[...] = o.astype(BF16)


def _diff_attn(qk, u, lq1, lk1, lq2, lk2, subln_g, *, batch, seq, tq):
    m = u.shape[0]
    nq = seq // tq
    vec = pl.BlockSpec((1, DIFF_HEAD_DIM), lambda b, h, i: (0, 0))
    kernel = functools.partial(_attn_kernel, tq=tq)
    return pl.pallas_call(
        kernel,
        out_shape=jax.ShapeDtypeStruct((m, DIFF_HEADS * DIFF_V_DIM), BF16),
        grid=(batch, DIFF_HEADS, nq),
        in_specs=[
            vec, vec, vec, vec,
            pl.BlockSpec((1, DIFF_V_DIM), lambda b, h, i: (0, 0)),
            pl.BlockSpec((1, tq, 2 * DIFF_HEAD_DIM), lambda b, h, i: (0, b * nq + i, h)),
            pl.BlockSpec((1, seq, 2 * DIFF_HEAD_DIM), lambda b, h, i: (1, b, h)),
            pl.BlockSpec((seq, DIFF_V_DIM), lambda b, h, i: (b, COL_DV // DIFF_V_DIM + h)),
        ],
        out_specs=pl.BlockSpec((tq, DIFF_V_DIM), lambda b, h, i: (b * nq + i, h)),
        scratch_shapes=[pltpu.VMEM((2, tq, 1), F32), pltpu.VMEM((2, tq, 1), F32),
                        pltpu.VMEM((2, tq, DIFF_V_DIM), F32)],
        compiler_params=_params(("parallel", "parallel", "arbitrary")),
        name="diff_attn",
    )(lq1, lk1, lq2, lk2, subln_g, qk, qk, u)


def _branch_kernel(og_ref, od_ref, wg_ref, wd_ref, mg_ref, md_ref, o_ref):
    yg = jnp.dot(og_ref[...], wg_ref[...], preferred_element_type=F32)
    yd = jnp.dot(od_ref[...], wd_ref[...], preferred_element_type=F32)
    mixed = _sigmoid(mg_ref[...].astype(F32)) * yg + _sigmoid(md_ref[...].astype(F32)) * yd
    o_ref[...] = mixed.astype(BF16)


def _branch(o_gla, o_diff, w_branch, u, *, tm, tn):
    m = u.shape[0]
    nrow = GLA_V_WIDTH // D_MODEL
    return pl.pallas_call(
        _branch_kernel,
        out_shape=jax.ShapeDtypeStruct((m, D_MODEL), BF16),
        grid=(m // tm, D_MODEL // tn),
        in_specs=[
            pl.BlockSpec((tm, GLA_V_WIDTH), lambda i, j: (i, 0)),
            pl.BlockSpec((tm, D_MODEL), lambda i, j: (i, 0)),
            pl.BlockSpec((GLA_V_WIDTH, tn), lambda i, j: (0, j)),
            pl.BlockSpec((D_MODEL, tn), lambda i, j: (nrow, j)),
            pl.BlockSpec((tm, tn), lambda i, j: (i, COL_MGLA // tn + j)),
            pl.BlockSpec((tm, tn), lambda i, j: (i, COL_MDIFF // tn + j)),
        ],
        out_specs=pl.BlockSpec((tm, tn), lambda i, j: (i, j)),
        compiler_params=_params(("parallel", "arbitrary")),
        name="branch",
    )(o_gla, o_diff, w_branch, w_branch, u, u)


def _outproj_kernel(mixed_ref, x_ref, w_ref, g_ref, x1_ref, h2_ref):
    x1 = x_ref[...] + jnp.dot(mixed_ref[...], w_ref[...], preferred_element_type=F32)
    x1_ref[...] = x1
    h2_ref[...] = (_rms_scale(x1, EPS) * g_ref[...]).astype(BF16)


def _out_proj(mixed, x2d, w_out, g2, *, tm):
    m = x2d.shape[0]
    return pl.pallas_call(
        _outproj_kernel,
        out_shape=(jax.ShapeDtypeStruct((m, D_MODEL), F32),
                   jax.ShapeDtypeStruct((m, D_MODEL), BF16)),
        grid=(m // tm,),
        in_specs=[
            pl.BlockSpec((tm, D_MODEL), lambda i: (i, 0)),
            pl.BlockSpec((tm, D_MODEL), lambda i: (i, 0)),
            pl.BlockSpec((D_MODEL, D_MODEL), lambda i: (0, 0)),
            pl.BlockSpec((1, D_MODEL), lambda i: (0, 0)),
        ],
        out_specs=(pl.BlockSpec((tm, D_MODEL), lambda i: (i, 0)),
                   pl.BlockSpec((tm, D_MODEL), lambda i: (i, 0))),
        compiler_params=_params(("parallel",)),
        name="out_proj",
    )(mixed, x2d, w_out, g2)


def _ffn_up_kernel(h2_ref, wa_ref, wv_ref, cw_ref, cb_ref, act_ref, carry_scr, *, tiles_per_seq):
    @pl.when(pl.program_id(1) % tiles_per_seq == 0)
    def _():
        carry_scr[...] = jnp.zeros_like(carry_scr)

    h2 = h2_ref[...]
    a = jnp.dot(h2, wa_ref[...], preferred_element_type=F32)
    val = jnp.dot(h2, wv_ref[...], preferred_element_type=F32)
    tm = a.shape[0]
    carry = carry_scr[...]
    row = lax.broadcasted_iota(jnp.int32, carry.shape, 0)

    def shifted(k):
        r = pltpu.roll(a, k, 0)
        head = jnp.where(row < k, pltpu.roll(carry, k, 0), r[:SUBLANES])
        return jnp.concatenate([head, r[SUBLANES:]], axis=0)

    conv = cb_ref[...] + shifted(2) * cw_ref[0:1, :] + shifted(1) * cw_ref[1:2, :] + a * cw_ref[2:3, :]
    carry_scr[...] = a[tm - SUBLANES:]
    act_ref[...] = (conv * _sigmoid(conv) * val).astype(BF16)


def _ffn_up(h2, w_up, conv_w, conv_b, *, seq, tm, tf):
    m = h2.shape[0]
    nf = FFN_HIDDEN // tf
    kernel = functools.partial(_ffn_up_kernel, tiles_per_seq=seq // tm)
    return pl.pallas_call(
        kernel,
        out_shape=jax.ShapeDtypeStruct((m, FFN_HIDDEN), BF16),
        grid=(nf, m // tm),
        in_specs=[
            pl.BlockSpec((tm, D_MODEL), lambda f, i: (i, 0)),
            pl.BlockSpec((D_MODEL, tf), lambda f, i: (0, f)),
            pl.BlockSpec((D_MODEL, tf), lambda f, i: (0, nf + f)),
            pl.BlockSpec((CONV_WIDTH, tf), lambda f, i: (0, f)),
            pl.BlockSpec((1, tf), lambda f, i: (0, f)),
        ],
        out_specs=pl.BlockSpec((tm, tf), lambda f, i: (i, f)),
        scratch_shapes=[pltpu.VMEM((SUBLANES, tf), F32)],
        compiler_params=_params(("parallel", "arbitrary")),
        name="ffn_up",
    )(h2, w_up, w_up, conv_w, conv_b)


def _ffn_down_kernel(act_ref, w_ref, x1_ref, o_ref):
    @pl.when(pl.program_id(1) == 0)
    def _():
        o_ref[...] = x1_ref[...]

    o_ref[...] += jnp.dot(act_ref[...], w_ref[...], preferred_element_type=F32)


def _ffn_down(act, w_down, x1, *, tm, tk):
    m = x1.shape[0]
    return pl.pallas_call(
        _ffn_down_kernel,
        out_shape=jax.ShapeDtypeStruct((m, D_MODEL), F32),
        grid=(m // tm, FFN_HIDDEN // tk),
        in_specs=[
            pl.BlockSpec((tm, tk), lambda i, k: (i, k)),
            pl.BlockSpec((tk, D_MODEL), lambda i, k: (k, 0)),
            pl.BlockSpec((tm, D_MODEL), lambda i, k: (i, 0)),
        ],
        out_specs=pl.BlockSpec((tm, D_MODEL), lambda i, k: (i, 0)),
        compiler_params=_params(("parallel", "arbitrary")),
        name="ffn_down",
    )(act, w_down, x1)


def _rope_tables():
    inv_freq = ROPE_THETA ** (-jnp.arange(0, ROPE_DIM, 2, dtype=F32) / ROPE_DIM)
    pad = jnp.zeros((DIFF_HEAD_DIM - ROPE_DIM,), F32)
    freq = jnp.concatenate([inv_freq, inv_freq, pad])[None, :]
    half = jnp.ones((ROPE_DIM // 2,), F32)
    sgn = jnp.concatenate([-half, half, pad])[None, :]
    return freq, sgn


def kernel(x, positions, norm1_g, w_in, w_gk_up, b_gk, gla_norm_g, q_norm_g, k_norm_g, lambda_q1,
           lambda_k1, lambda_q2, lambda_k2, subln_g, w_branch, w_out, norm2_g, w_ffn_up, conv_w,
           conv_b, w_ffn_down):
    batch, seq, _ = x.shape
    m = batch * seq
    l = 0
    x2d = x.reshape(m, D_MODEL)
    pos2d = positions.reshape(m, 1)

    w_packed = jnp.concatenate([w_in[l][:, :LR_START], w_in[l][:, LR_END:]], axis=1).astype(BF16)
    w_lr = jnp.pad(w_in[l][:, LR_START:LR_END], ((0, 0), (0, LANES - GLA_GATE_RANK))).astype(BF16)
    wup_pad = jnp.pad(w_gk_up[l], ((0, LANES - GLA_GATE_RANK), (0, 0)))
    qk_gain = jnp.stack([q_norm_g[l], k_norm_g[l]])[:, None, :]
    freq, sgn = _rope_tables()

    u, glr = _in_proj(x2d, norm1_g[l][None, :], w_packed, w_lr, tm=1024, tn=1024)
    qk = _qk_prep(u, pos2d, qk_gain, freq, sgn, tm=512)
    o_gla = _gla(u, glr, wup_pad, b_gk[l][None, :], gla_norm_g[l][None, :], batch=batch, seq=seq, tb=512)
    o_diff = _diff_attn(qk, u, lambda_q1[l][None, :], lambda_k1[l][None, :], lambda_q2[l][None, :],
                        lambda_k2[l][None, :], subln_g[l][None, :], batch=batch, seq=seq, tq=512)
    mixed = _branch(o_gla, o_diff, w_branch[l].astype(BF16), u, tm=1024, tn=512)
    x1, h2 = _out_proj(mixed, x2d, w_out[l].astype(BF16), norm2_g[l][None, :], tm=512)
    act = _ffn_up(h2, w_ffn_up[l].astype(BF16), conv_w[l], conv_b[l][None, :], seq=seq, tm=1024, tf=512)
    out = _ffn_down(act, w_ffn_down[l].astype(BF16), x1, tm=1024, tk=512)
    return out.reshape(batch, seq, D_MODEL)
```

```python
import functools
import math

import jax
import jax.numpy as jnp
from jax import lax
from jax.experimental import pallas as pl
from jax.experimental.pallas import tpu as pltpu

F32 = jnp.float32
BF16 = jnp.bfloat16

D_MODEL = 2048
GLA_HEADS = 4
GLA_DK = 256
GLA_DV = 512
GLA_K_WIDTH = GLA_HEADS * GLA_DK
GLA_V_WIDTH = GLA_HEADS * GLA_DV
GLA_GATE_RANK = 16
GLA_GATE_NORMALIZER = 16.0
GLA_CHUNK = 64
DIFF_HEAD_DIM = 128
DIFF_HEADS = 8
DIFF_V_DIM = 256
ROPE_THETA = 500000.0
ROPE_DIM = 32
FFN_HIDDEN = 5632
CONV_WIDTH = 3
EPS = 1e-6
SUBLN_EPS = 1e-5
LAMBDA_INIT = 0.8 - 0.6 * math.exp(-0.3 * 0)
LOG2_E = math.log2(math.e)

LR_START = 2 * GLA_K_WIDTH + GLA_V_WIDTH
LR_END = LR_START + GLA_GATE_RANK
PACKED_WIDTH = 16384
COL_GQ, COL_GK, COL_GV, COL_GOUT = 0, 1024, 2048, 4096
COL_DQ, COL_DK, COL_DV, COL_MGLA, COL_MDIFF = 6144, 8192, 10240, 12288, 14336

LANES = 128
SUBLANES = 8
MXU_DIM = 256
VMEM_LIMIT_BYTES = 56 * 1024 * 1024

NT_DIMS = (((1,), (1,)), ((), ()))
TN_DIMS = (((0,), (0,)), ((), ()))


def _params(semantics):
    return pltpu.CompilerParams(dimension_semantics=semantics, vmem_limit_bytes=VMEM_LIMIT_BYTES)


def _sigmoid(x):
    return 1.0 / (1.0 + jnp.exp(-x))


def _log_sigmoid(x):
    return jnp.minimum(x, 0.0) - jnp.log1p(jnp.exp(-jnp.abs(x)))


def _rms_scale(x, eps):
    return x * lax.rsqrt(jnp.mean(x * x, axis=-1, keepdims=True) + eps)


def _split_bf16(x):
    hi = x.astype(BF16)
    lo = (x - hi.astype(F32)).astype(BF16)
    return hi, lo


def _dot(a, b):
    return jnp.dot(a, b, preferred_element_type=F32)


def _norm1_kernel(x_ref, g_ref, wlr_ref, h_ref, glr_ref):
    h = (_rms_scale(x_ref[...], EPS) * g_ref[...]).astype(BF16)
    h_ref[...] = h
    glr_ref[...] = _dot(h, wlr_ref[...].astype(BF16))


def _norm1(x2d, g, w_in2d, *, tm):
    m = x2d.shape[0]
    return pl.pallas_call(
        _norm1_kernel,
        out_shape=(jax.ShapeDtypeStruct((m, D_MODEL), BF16),
                   jax.ShapeDtypeStruct((m, LANES), F32)),
        grid=(m // tm,),
        in_specs=[
            pl.BlockSpec((tm, D_MODEL), lambda i: (i, 0)),
            pl.BlockSpec((1, D_MODEL), lambda i: (0, 0)),
            pl.BlockSpec((D_MODEL, LANES), lambda i: (0, LR_START // LANES)),
        ],
        out_specs=(pl.BlockSpec((tm, D_MODEL), lambda i: (i, 0)),
                   pl.BlockSpec((tm, LANES), lambda i: (i, 0))),
        compiler_params=_params(("parallel",)),
        name="norm1",
    )(x2d, g, w_in2d)


def _rope_kernel(pos_ref, freq_ref, sgn_ref, cos_ref, sin_ref):
    ang = pos_ref[...].astype(F32) * freq_ref[...]
    cos_ref[...] = jnp.cos(ang)
    sin_ref[...] = jnp.sin(ang) * sgn_ref[...]


def _rope_tables(pos2d, *, tm):
    m = pos2d.shape[0]
    inv_freq = ROPE_THETA ** (-jnp.arange(0, ROPE_DIM, 2, dtype=F32) / ROPE_DIM)
    pad = jnp.zeros((DIFF_HEAD_DIM - ROPE_DIM,), F32)
    freq = jnp.concatenate([inv_freq, inv_freq, pad])[None, :]
    half = jnp.ones((ROPE_DIM // 2,), F32)
    sgn = jnp.concatenate([-half, half, pad])[None, :]
    vec = pl.BlockSpec((1, DIFF_HEAD_DIM), lambda i: (0, 0))
    tab = pl.BlockSpec((tm, DIFF_HEAD_DIM), lambda i: (i, 0))
    return pl.pallas_call(
        _rope_kernel,
        out_shape=(jax.ShapeDtypeStruct((m, DIFF_HEAD_DIM), F32),) * 2,
        grid=(m // tm,),
        in_specs=[pl.BlockSpec((tm, 1), lambda i: (i, 0)), vec, vec],
        out_specs=(tab, tab),
        compiler_params=_params(("parallel",)),
        name="rope_tab",
    )(pos2d, freq, sgn)


def _inproj_kernel(h_ref, wa_ref, wb_ref, cos_ref, sin_ref, qkg_ref, u_ref, w_scr, *, tn):
    j = pl.program_id(0)
    first = pl.program_id(1) == 0
    shifted = j >= LR_START // tn

    @pl.when(first & jnp.logical_not(shifted))
    def _():
        w_scr[...] = wa_ref[...].astype(BF16)

    @pl.when(first & shifted)
    def _():
        r = GLA_GATE_RANK
        a_sh = pltpu.roll(wa_ref[...], tn - r, 1)
        b_sh = pltpu.roll(wb_ref[...], LANES - r, 1)
        lane = lax.broadcasted_iota(jnp.int32, b_sh.shape, 1)
        last = jnp.where(lane >= LANES - r, b_sh, a_sh[:, tn - LANES:])
        w_scr[:, :tn - LANES] = a_sh[:, :tn - LANES].astype(BF16)
        w_scr[:, tn - LANES:] = last.astype(BF16)

    acc = _dot(h_ref[...], w_scr[...])
    col0 = j * tn
    is_q = (col0 >= COL_DQ) & (col0 < COL_DK)
    is_k = (col0 >= COL_DK) & (col0 < COL_DV)

    @pl.when(jnp.logical_not(is_q | is_k))
    def _():
        u_ref[...] = acc.astype(BF16)

    @pl.when(is_q | is_k)
    def _():
        scale = jnp.where(is_q, LOG2_E * DIFF_HEAD_DIM ** -0.5, 1.0).astype(F32)
        g = jnp.where(is_q, qkg_ref[0], qkg_ref[1])
        cosf = cos_ref[...] * scale
        sinf = sin_ref[...] * scale
        first_half = lax.broadcasted_iota(jnp.int32, cosf.shape, 1) < ROPE_DIM // 2
        for hh in range(tn // DIFF_HEAD_DIM):
            cols = slice(hh * DIFF_HEAD_DIM, (hh + 1) * DIFF_HEAD_DIM)
            y = _rms_scale(acc[:, cols], EPS) * g
            partner = jnp.where(first_half,
                                pltpu.roll(y, DIFF_HEAD_DIM - ROPE_DIM // 2, 1),
                                pltpu.roll(y, ROPE_DIM // 2, 1))
            u_ref[:, cols] = (y * cosf + partner * sinf).astype(BF16)


def _in_proj(h, w_in2d, cos_tab, sin_tab, qk_gain, *, tm, tn):
    m = h.shape[0]
    kernel = functools.partial(_inproj_kernel, tn=tn)
    tab = pl.BlockSpec((tm, DIFF_HEAD_DIM), lambda j, i: (i, 0))

    def wa_map(j, i):
        return (0, j)

    def wb_map(j, i):
        return (0, (j + 1) * (tn // LANES))

    return pl.pallas_call(
        kernel,
        out_shape=jax.ShapeDtypeStruct((m, PACKED_WIDTH), BF16),
        grid=(PACKED_WIDTH // tn, m // tm),
        in_specs=[
            pl.BlockSpec((tm, D_MODEL), lambda j, i: (i, 0)),
            pl.BlockSpec((D_MODEL, tn), wa_map),
            pl.BlockSpec((D_MODEL, LANES), wb_map),
            tab, tab,
            pl.BlockSpec((2, 1, DIFF_HEAD_DIM), lambda j, i: (0, 0, 0)),
        ],
        out_specs=pl.BlockSpec((tm, tn), lambda j, i: (i, j)),
        scratch_shapes=[pltpu.VMEM((D_MODEL, tn), BF16)],
        compiler_params=_params(("parallel", "arbitrary")),
        name="in_proj",
    )(h, w_in2d, w_in2d, cos_tab, sin_tab, qk_gain)


def _gla_kernel(q_ref, k_ref, v_ref, go_ref, glr_ref, wup_ref, bgk_ref, gn_ref, o_ref, s_scr, *, tb):
    @pl.when(pl.program_id(2) == 0)
    def _():
        s_scr[...] = jnp.zeros_like(s_scr)

    c = GLA_CHUNK
    x_hi, x_lo = _split_bf16(glr_ref[...])
    w_hi, w_lo = _split_bf16(wup_ref[...])
    gk = _dot(x_hi, w_hi) + _dot(x_hi, w_lo) + _dot(x_lo, w_hi) + bgk_ref[...]
    log_decay = _log_sigmoid(gk) / GLA_GATE_NORMALIZER
    ld_hi, ld_lo = _split_bf16(log_decay)

    ri = lax.broadcasted_iota(jnp.int32, (c, c), 0)
    ci = lax.broadcasted_iota(jnp.int32, (c, c), 1)
    causal = ri >= ci
    tri = jnp.where(causal, 1.0, 0.0).astype(BF16)
    gn = gn_ref[...]
    for n in range(tb // c):
        rows = slice(n * c, (n + 1) * c)
        b = _dot(tri, ld_hi[rows]) + _dot(tri, ld_lo[rows])
        b_last = b[c - 1:c]
        q = q_ref[rows, :].astype(F32) * (GLA_DK ** -0.5)
        k = k_ref[rows, :].astype(F32)
        v = v_ref[rows, :]
        q_dec = (q * jnp.exp(b)).astype(BF16)
        k_inv = (k * jnp.exp(-b)).astype(BF16)
        k_tail = (k * jnp.exp(b_last - b)).astype(BF16)
        a = lax.dot_general(q_dec, k_inv, NT_DIMS, preferred_element_type=F32)
        a = jnp.where(causal, a, 0.0).astype(BF16)
        s_in = s_scr[...]
        o = _dot(a, v) + lax.dot_general(q_dec, s_in.astype(BF16), NT_DIMS,
                                         preferred_element_type=F32)
        kv_t = lax.dot_general(v, k_tail, TN_DIMS, preferred_element_type=F32)
        s_scr[...] = s_in * jnp.exp(b_last) + kv_t
        go = go_ref[rows, :].astype(F32)
        o = _rms_scale(o, EPS) * gn * (go * _sigmoid(go))
        o_ref[rows, :] = o.astype(BF16)


def _gla(u, glr, wup_pad, b_gk, gla_norm_g, *, batch, seq, tb):
    m = u.shape[0]
    nt = seq // tb
    kernel = functools.partial(_gla_kernel, tb=tb)
    return pl.pallas_call(
        kernel,
        out_shape=jax.ShapeDtypeStruct((m, GLA_V_WIDTH), BF16),
        grid=(batch, GLA_HEADS, nt),
        in_specs=[
            pl.BlockSpec((tb, GLA_DK), lambda b, h, t: (b * nt + t, COL_GQ // GLA_DK + h)),
            pl.BlockSpec((tb, GLA_DK), lambda b, h, t: (b * nt + t, COL_GK // GLA_DK + h)),
            pl.BlockSpec((tb, GLA_DV), lambda b, h, t: (b * nt + t, COL_GV // GLA_DV + h)),
            pl.BlockSpec((tb, GLA_DV), lambda b, h, t: (b * nt + t, COL_GOUT // GLA_DV + h)),
            pl.BlockSpec((tb, LANES), lambda b, h, t: (b * nt + t, 0)),
            pl.BlockSpec((LANES, GLA_DK), lambda b, h, t: (0, h)),
            pl.BlockSpec((1, GLA_DK), lambda b, h, t: (0, h)),
            pl.BlockSpec((1, GLA_DV), lambda b, h, t: (0, 0)),
        ],
        out_specs=pl.BlockSpec((tb, GLA_DV), lambda b, h, t: (b * nt + t, h)),
        scratch_shapes=[pltpu.VMEM((GLA_DV, GLA_DK), F32)],
        compiler_params=_params(("parallel", "parallel", "arbitrary")),
        name="gla",
    )(u, u, u, u, glr, wup_pad, b_gk, gla_norm_g)


def _attn_kernel(lq1_ref, lk1_ref, lq2_ref, lk2_ref, sg_ref, q_ref, k_ref, v_ref, o_ref,
                 m_scr, l_scr, acc_scr, *, tq, tk):
    qi = pl.program_id(2)
    d = DIFF_HEAD_DIM
    q = q_ref[...]
    qs = (q[:, :d], q[:, d:])
    row0 = qi * tq

    m_scr[...] = jnp.full_like(m_scr, -jnp.inf)
    l_scr[...] = jnp.zeros_like(l_scr)
    acc_scr[...] = jnp.zeros_like(acc_scr)

    def step(j, masked):
        start = pl.multiple_of(j * tk, tk)
        kblk = k_ref[pl.ds(start, tk), :]
        vblk = v_ref[pl.ds(start, tk), :]
        for sub in range(2):
            cols = []
            for n in range(tk // MXU_DIM):
                ksub = kblk[n * MXU_DIM:(n + 1) * MXU_DIM, sub * d:(sub + 1) * d]
                s = lax.dot_general(qs[sub], ksub, NT_DIMS, preferred_element_type=F32)
                if masked:
                    keep = (row0 + lax.broadcasted_iota(jnp.int32, s.shape, 0)
                            >= start + n * MXU_DIM + lax.broadcasted_iota(jnp.int32, s.shape, 1))
                    s = jnp.where(keep, s, -jnp.inf)
                cols += [s[:, c * LANES:(c + 1) * LANES] for c in range(MXU_DIM // LANES)]
            m_old = m_scr[sub]
            row_max = jnp.max(functools.reduce(jnp.maximum, cols), axis=-1, keepdims=True)
            m_new = jnp.maximum(m_old, row_max)
            alpha = jnp.exp2(m_old - m_new)
            ps = [jnp.exp2(col - m_new) for col in cols]
            l_scr[sub] = alpha * l_scr[sub] + functools.reduce(jnp.add, ps)
            p = jnp.concatenate(ps, axis=1).astype(BF16)
            alpha2 = jnp.concatenate([alpha, alpha], axis=1)
            for r in range(tq // MXU_DIM):
                rows = slice(r * MXU_DIM, (r + 1) * MXU_DIM)
                pv = _dot(p[rows], vblk)
                acc_scr[sub, rows, :] = alpha2[rows] * acc_scr[sub, rows, :] + pv
            m_scr[sub] = m_new

    def body(j, carry):
        step(j, masked=False)
        return carry

    n_full = (qi * tq) // tk
    n_all = ((qi + 1) * tq + tk - 1) // tk
    lax.fori_loop(0, n_full, body, 0)
    for extra in range(max(tq // tk, 1)):
        @pl.when(n_full + extra < n_all)
        def _():
            step(n_full + extra, masked=True)

    lam = (jnp.exp(jnp.sum(lq1_ref[...] * lk1_ref[...], axis=-1, keepdims=True))
           - jnp.exp(jnp.sum(lq2_ref[...] * lk2_ref[...], axis=-1, keepdims=True))
           + LAMBDA_INIT)
    l1 = jnp.sum(l_scr[0], axis=-1, keepdims=True)
    l2 = jnp.sum(l_scr[1], axis=-1, keepdims=True)
    o = acc_scr[0] / l1 - lam * (acc_scr[1] / l2)
    o = _rms_scale(o, SUBLN_EPS) * sg_ref[...] * (1.0 - LAMBDA_INIT)
    o_ref[...] = o.astype(BF16)


def _diff_attn(u, lq1, lk1, lq2, lk2, subln_g, *, batch, seq, tq, tk):
    m = u.shape[0]
    nq = seq // tq
    width = 2 * DIFF_HEAD_DIM
    vec = pl.BlockSpec((1, DIFF_HEAD_DIM), lambda b, h, i: (0, 0))
    kernel = functools.partial(_attn_kernel, tq=tq, tk=tk)
    return pl.pallas_call(
        kernel,
        out_shape=jax.ShapeDtypeStruct((m, DIFF_HEADS * DIFF_V_DIM), BF16),
        grid=(batch, DIFF_HEADS, nq),
        in_specs=[
            vec, vec, vec, vec,
            pl.BlockSpec((1, DIFF_V_DIM), lambda b, h, i: (0, 0)),
            pl.BlockSpec((tq, width), lambda b, h, i: (b * nq + i, COL_DQ // width + h)),
            pl.BlockSpec((seq, width), lambda b, h, i: (b, COL_DK // width + h)),
            pl.BlockSpec((seq, DIFF_V_DIM), lambda b, h, i: (b, COL_DV // DIFF_V_DIM + h)),
        ],
        out_specs=pl.BlockSpec((tq, DIFF_V_DIM), lambda b, h, i: (b * nq + i, h)),
        scratch_shapes=[pltpu.VMEM((2, tq, LANES), F32), pltpu.VMEM((2, tq, LANES), F32),
                        pltpu.VMEM((2, tq, DIFF_V_DIM), F32)],
        compiler_params=_params(("parallel", "parallel", "arbitrary")),
        name="diff_attn",
    )(lq1, lk1, lq2, lk2, subln_g, u, u, u)


def _branch_kernel(og_ref, od_ref, wg_ref, wd_ref, mg_ref, md_ref, o_ref, wg_scr, wd_scr):
    @pl.when(pl.program_id(1) == 0)
    def _():
        wg_scr[...] = wg_ref[...].astype(BF16)
        wd_scr[...] = wd_ref[...].astype(BF16)

    yg = _dot(og_ref[...], wg_scr[...])
    yd = _dot(od_ref[...], wd_scr[...])
    mixed = _sigmoid(mg_ref[...].astype(F32)) * yg + _sigmoid(md_ref[...].astype(F32)) * yd
    o_ref[...] = mixed.astype(BF16)


def _branch(o_gla, o_diff, w_branch, u, *, tm, tn):
    m = u.shape[0]
    nrow = GLA_V_WIDTH // D_MODEL
    return pl.pallas_call(
        _branch_kernel,
        out_shape=jax.ShapeDtypeStruct((m, D_MODEL), BF16),
        grid=(D_MODEL // tn, m // tm),
        in_specs=[
            pl.BlockSpec((tm, GLA_V_WIDTH), lambda j, i: (i, 0)),
            pl.BlockSpec((tm, D_MODEL), lambda j, i: (i, 0)),
            pl.BlockSpec((GLA_V_WIDTH, tn), lambda j, i: (0, j)),
            pl.BlockSpec((D_MODEL, tn), lambda j, i: (nrow, j)),
            pl.BlockSpec((tm, tn), lambda j, i: (i, COL_MGLA // tn + j)),
            pl.BlockSpec((tm, tn), lambda j, i: (i, COL_MDIFF // tn + j)),
        ],
        out_specs=pl.BlockSpec((tm, tn), lambda j, i: (i, j)),
        scratch_shapes=[pltpu.VMEM((GLA_V_WIDTH, tn), BF16), pltpu.VMEM((D_MODEL, tn), BF16)],
        compiler_params=_params(("parallel", "arbitrary")),
        name="branch",
    )(o_gla, o_diff, w_branch, w_branch, u, u)


def _outproj_kernel(mixed_ref, x_ref, w_ref, g_ref, x1_ref, h2_ref):
    x1 = x_ref[...] + _dot(mixed_ref[...], w_ref[...])
    x1_ref[...] = x1
    h2_ref[...] = (_rms_scale(x1, EPS) * g_ref[...]).astype(BF16)


def _out_proj(mixed, x2d, w_out, g2, *, tm):
    m = x2d.shape[0]
    return pl.pallas_call(
        _outproj_kernel,
        out_shape=(jax.ShapeDtypeStruct((m, D_MODEL), F32),
                   jax.ShapeDtypeStruct((m, D_MODEL), BF16)),
        grid=(m // tm,),
        in_specs=[
            pl.BlockSpec((tm, D_MODEL), lambda i: (i, 0)),
            pl.BlockSpec((tm, D_MODEL), lambda i: (i, 0)),
            pl.BlockSpec((D_MODEL, D_MODEL), lambda i: (0, 0)),
            pl.BlockSpec((1, D_MODEL), lambda i: (0, 0)),
        ],
        out_specs=(pl.BlockSpec((tm, D_MODEL), lambda i: (i, 0)),
                   pl.BlockSpec((tm, D_MODEL), lambda i: (i, 0))),
        compiler_params=_params(("parallel",)),
        name="out_proj",
    )(mixed, x2d, w_out, g2)


def _ffn_up_kernel(h2_ref, wa_ref, wv_ref, cw_ref, cb_ref, act_ref, wa_scr, wv_scr, carry_scr, *,
                   tiles_per_seq):
    i = pl.program_id(1)

    @pl.when(i == 0)
    def _():
        wa_scr[...] = wa_ref[...].astype(BF16)
        wv_scr[...] = wv_ref[...].astype(BF16)

    @pl.when(i % tiles_per_seq == 0)
    def _():
        carry_scr[...] = jnp.zeros_like(carry_scr)

    h2 = h2_ref[...]
    a = _dot(h2, wa_scr[...])
    val = _dot(h2, wv_scr[...])
    tm = a.shape[0]
    carry = carry_scr[...]
    row = lax.broadcasted_iota(jnp.int32, carry.shape, 0)

    def shifted(k):
        r = pltpu.roll(a, k, 0)
        head = jnp.where(row < k, pltpu.roll(carry, k, 0), r[:SUBLANES])
        return jnp.concatenate([head, r[SUBLANES:]], axis=0)

    conv = cb_ref[...] + shifted(2) * cw_ref[0:1, :] + shifted(1) * cw_ref[1:2, :] + a * cw_ref[2:3, :]
    carry_scr[...] = a[tm - SUBLANES:]
    act_ref[...] = (conv * _sigmoid(conv) * val).astype(BF16)


def _ffn_up(h2, w_up, conv_w, conv_b, *, seq, tm, tf):
    m = h2.shape[0]
    nf = FFN_HIDDEN // tf
    kernel = functools.partial(_ffn_up_kernel, tiles_per_seq=seq // tm)
    return pl.pallas_call(
        kernel,
        out_shape=jax.ShapeDtypeStruct((m, FFN_HIDDEN), BF16),
        grid=(nf, m // tm),
        in_specs=[
            pl.BlockSpec((tm, D_MODEL), lambda f, i: (i, 0)),
            pl.BlockSpec((D_MODEL, tf), lambda f, i: (0, f)),
            pl.BlockSpec((D_MODEL, tf), lambda f, i: (0, nf + f)),
            pl.BlockSpec((CONV_WIDTH, tf), lambda f, i: (0, f)),
            pl.BlockSpec((1, tf), lambda f, i: (0, f)),
        ],
        out_specs=pl.BlockSpec((tm, tf), lambda f, i: (i, f)),
        scratch_shapes=[pltpu.VMEM((D_MODEL, tf), BF16), pltpu.VMEM((D_MODEL, tf), BF16),
                        pltpu.VMEM((SUBLANES, tf), F32)],
        compiler_params=_params(("parallel", "arbitrary")),
        name="ffn_up",
    )(h2, w_up, w_up, conv_w, conv_b)


def _ffn_down_kernel(act_ref, w_ref, x1_ref, o_ref):
    @pl.when(pl.program_id(1) == 0)
    def _():
        o_ref[...] = x1_ref[...]

    o_ref[...] += _dot(act_ref[...], w_ref[...].astype(BF16))


def _ffn_down(act, w_down, x1, *, tm, tk):
    m = x1.shape[0]
    return pl.pallas_call(
        _ffn_down_kernel,
        out_shape=jax.ShapeDtypeStruct((m, D_MODEL), F32),
        grid=(m // tm, FFN_HIDDEN // tk),
        in_specs=[
            pl.BlockSpec((tm, tk), lambda i, k: (i, k)),
            pl.BlockSpec((tk, D_MODEL), lambda i, k: (k, 0)),
            pl.BlockSpec((tm, D_MODEL), lambda i, k: (i, 0)),
        ],
        out_specs=pl.BlockSpec((tm, D_MODEL), lambda i, k: (i, 0)),
        compiler_params=_params(("parallel", "arbitrary")),
        name="ffn_down",
    )(act, w_down, x1)


def kernel(x, positions, norm1_g, w_in, w_gk_up, b_gk, gla_norm_g, q_norm_g, k_norm_g, lambda_q1,
           lambda_k1, lambda_q2, lambda_k2, subln_g, w_branch, w_out, norm2_g, w_ffn_up, conv_w,
           conv_b, w_ffn_down):
    batch, seq, _ = x.shape
    m = batch * seq
    l = 0
    x2d = x.reshape(m, D_MODEL)
    pos2d = positions.reshape(m, 1)
    wup_pad = jnp.pad(w_gk_up[l], ((0, LANES - GLA_GATE_RANK), (0, 0)))
    qk_gain = jnp.stack([q_norm_g[l], k_norm_g[l]])[:, None, :]

    h, glr = _norm1(x2d, norm1_g[l][None, :], w_in[l], tm=512)
    cos_tab, sin_tab = _rope_tables(pos2d, tm=1024)
    u = _in_proj(h, w_in[l], cos_tab, sin_tab, qk_gain, tm=2048, tn=512)
    o_gla = _gla(u, glr, wup_pad, b_gk[l][None, :], gla_norm_g[l][None, :], batch=batch, seq=seq, tb=512)
    o_diff = _diff_attn(u, lambda_q1[l][None, :], lambda_k1[l][None, :], lambda_q2[l][None, :],
                        lambda_k2[l][None, :], subln_g[l][None, :], batch=batch, seq=seq, tq=512, tk=512)
    mixed = _branch(o_gla, o_diff, w_branch[l], u, tm=1024, tn=512)
    x1, h2 = _out_proj(mixed, x2d, w_out[l].astype(BF16), norm2_g[l][None, :], tm=512)
    act = _ffn_up(h2, w_ffn_up[l], conv_w[l], conv_b[l][None, :], seq=seq, tm=1024, tf=512)
    out = _ffn_down(act, w_ffn_down[l], x1, tm=1024, tk=512)
    return out.reshape(batch, seq, D_MODEL)
```

```python
import functools
import math

import jax
import jax.numpy as jnp
from jax import lax
from jax.experimental import pallas as pl
from jax.experimental.pallas import tpu as pltpu

F32 = jnp.float32
BF16 = jnp.bfloat16

D_MODEL = 2048
GLA_HEADS = 4
GLA_DK = 256
GLA_DV = 512
GLA_K_WIDTH = GLA_HEADS * GLA_DK
GLA_V_WIDTH = GLA_HEADS * GLA_DV
GLA_GATE_RANK = 16
GLA_GATE_NORMALIZER = 16.0
GLA_CHUNK = 64
DIFF_HEAD_DIM = 128
DIFF_HEADS = 8
DIFF_V_DIM = 256
ROPE_THETA = 500000.0
ROPE_DIM = 32
FFN_HIDDEN = 5632
CONV_WIDTH = 3
EPS = 1e-6
SUBLN_EPS = 1e-5
LAMBDA_INIT = 0.8 - 0.6 * math.exp(-0.3 * 0)
LOG2_E = math.log2(math.e)

LR_START = 2 * GLA_K_WIDTH + GLA_V_WIDTH
LR_END = LR_START + GLA_GATE_RANK
PACKED_WIDTH = 16384
COL_GQ, COL_GK, COL_GV, COL_GOUT = 0, 1024, 2048, 4096
COL_DQ, COL_DK, COL_DV, COL_MGLA, COL_MDIFF = 6144, 8192, 10240, 12288, 14336

LANES = 128
SUBLANES = 8
MXU_DIM = 256
VMEM_LIMIT_BYTES = 56 * 1024 * 1024

NT_DIMS = (((1,), (1,)), ((), ()))
TN_DIMS = (((0,), (0,)), ((), ()))


def _params(semantics):
    return pltpu.CompilerParams(dimension_semantics=semantics, vmem_limit_bytes=VMEM_LIMIT_BYTES)


def _sigmoid(x):
    return 1.0 / (1.0 + jnp.exp(-x))


def _log_sigmoid(x):
    return jnp.minimum(x, 0.0) - jnp.log1p(jnp.exp(-jnp.abs(x)))


def _rms_scale(x, eps):
    return x * lax.rsqrt(jnp.mean(x * x, axis=-1, keepdims=True) + eps)


def _split_bf16(x):
    hi = x.astype(BF16)
    lo = (x - hi.astype(F32)).astype(BF16)
    return hi, lo


def _dot(a, b):
    return jnp.dot(a, b, preferred_element_type=F32)


def _norm1_kernel(x_ref, g_ref, wlr_ref, h_ref, glr_ref):
    h = (_rms_scale(x_ref[...], EPS) * g_ref[...]).astype(BF16)
    h_ref[...] = h
    glr_ref[...] = _dot(h, wlr_ref[...].astype(BF16))


def _norm1(x2d, g, w_in2d, *, tm):
    m = x2d.shape[0]
    return pl.pallas_call(
        _norm1_kernel,
        out_shape=(jax.ShapeDtypeStruct((m, D_MODEL), BF16),
                   jax.ShapeDtypeStruct((m, LANES), F32)),
        grid=(m // tm,),
        in_specs=[
            pl.BlockSpec((tm, D_MODEL), lambda i: (i, 0)),
            pl.BlockSpec((1, D_MODEL), lambda i: (0, 0)),
            pl.BlockSpec((None, D_MODEL, LANES), lambda i: (0, 0, LR_START // LANES)),
        ],
        out_specs=(pl.BlockSpec((tm, D_MODEL), lambda i: (i, 0)),
                   pl.BlockSpec((tm, LANES), lambda i: (i, 0))),
        compiler_params=_params(("parallel",)),
        name="norm1",
    )(x2d, g, w_in2d)


def _rope_kernel(pos_ref, freq_ref, sgn_ref, cos_ref, sin_ref):
    ang = pos_ref[...].astype(F32) * freq_ref[...]
    cos_ref[...] = jnp.cos(ang)
    sin_ref[...] = jnp.sin(ang) * sgn_ref[...]


def _rope_tables(pos2d, *, tm):
    m = pos2d.shape[0]
    inv_freq = ROPE_THETA ** (-jnp.arange(0, ROPE_DIM, 2, dtype=F32) / ROPE_DIM)
    pad = jnp.zeros((DIFF_HEAD_DIM - ROPE_DIM,), F32)
    freq = jnp.concatenate([inv_freq, inv_freq, pad])[None, :]
    half = jnp.ones((ROPE_DIM // 2,), F32)
    sgn = jnp.concatenate([-half, half, pad])[None, :]
    vec = pl.BlockSpec((1, DIFF_HEAD_DIM), lambda i: (0, 0))
    tab = pl.BlockSpec((tm, DIFF_HEAD_DIM), lambda i: (i, 0))
    return pl.pallas_call(
        _rope_kernel,
        out_shape=(jax.ShapeDtypeStruct((m, DIFF_HEAD_DIM), F32),) * 2,
        grid=(m // tm,),
        in_specs=[pl.BlockSpec((tm, 1), lambda i: (i, 0)), vec, vec],
        out_specs=(tab, tab),
        compiler_params=_params(("parallel",)),
        name="rope_tab",
    )(pos2d, freq, sgn)


def _inproj_kernel(h_ref, wa_ref, wb_ref, cos_ref, sin_ref, qkg_ref, u_ref, *, tn, rc):
    col0 = pl.program_id(1) * tn
    is_q = (col0 >= COL_DQ) & (col0 < COL_DK)
    is_k = (col0 >= COL_DK) & (col0 < COL_DV)

    def weights(r):
        a_sh = pltpu.roll(wa_ref[...], (tn - r) % tn, 1)
        b_sh = pltpu.roll(wb_ref[...], (LANES - r) % LANES, 1)
        lane = lax.broadcasted_iota(jnp.int32, b_sh.shape, 1)
        last = jnp.where(lane >= LANES - r, b_sh, a_sh[:, tn - LANES:])
        return jnp.concatenate([a_sh[:, :tn - LANES], last], axis=1).astype(BF16)

    @pl.when(jnp.logical_not(is_q | is_k))
    def _():
        r = jnp.where(col0 >= LR_START, GLA_GATE_RANK, 0)
        u_ref[...] = _dot(h_ref[...], weights(r)).astype(BF16)

    @pl.when(is_q | is_k)
    def _():
        w = weights(GLA_GATE_RANK)
        scale = jnp.where(is_q, LOG2_E * DIFF_HEAD_DIM ** -0.5, 1.0).astype(F32)
        g = jnp.where(is_q, qkg_ref[0], qkg_ref[1])
        first_half = lax.broadcasted_iota(jnp.int32, (rc, DIFF_HEAD_DIM), 1) < ROPE_DIM // 2
        for c in range(h_ref.shape[0] // rc):
            rows = slice(c * rc, (c + 1) * rc)
            acc = _dot(h_ref[rows, :], w)
            cosf = cos_ref[rows, :] * scale
            sinf = sin_ref[rows, :] * scale
            for hh in range(tn // DIFF_HEAD_DIM):
                cols = slice(hh * DIFF_HEAD_DIM, (hh + 1) * DIFF_HEAD_DIM)
                y = _rms_scale(acc[:, cols], EPS) * g
                partner = jnp.where(first_half,
                                    pltpu.roll(y, DIFF_HEAD_DIM - ROPE_DIM // 2, 1),
                                    pltpu.roll(y, ROPE_DIM // 2, 1))
                u_ref[rows, cols] = (y * cosf + partner * sinf).astype(BF16)


def _in_proj(h, w_in, cos_tab, sin_tab, qk_gain, *, tm, tn, rc):
    m = h.shape[0]
    kernel = functools.partial(_inproj_kernel, tn=tn, rc=rc)
    tab = pl.BlockSpec((tm, DIFF_HEAD_DIM), lambda i, j: (i, 0))
    return pl.pallas_call(
        kernel,
        out_shape=jax.ShapeDtypeStruct((m, PACKED_WIDTH), BF16),
        grid=(m // tm, PACKED_WIDTH // tn),
        in_specs=[
            pl.BlockSpec((tm, D_MODEL), lambda i, j: (i, 0)),
            pl.BlockSpec((None, D_MODEL, tn), lambda i, j: (0, 0, j)),
            pl.BlockSpec((None, D_MODEL, LANES), lambda i, j: (0, 0, (j + 1) * (tn // LANES))),
            tab, tab,
            pl.BlockSpec((2, 1, DIFF_HEAD_DIM), lambda i, j: (0, 0, 0)),
        ],
        out_specs=pl.BlockSpec((tm, tn), lambda i, j: (i, j)),
        compiler_params=_params(("parallel", "arbitrary")),
        name="in_proj",
    )(h, w_in, w_in, cos_tab, sin_tab, qk_gain)


def _gla_kernel(q_ref, k_ref, v_ref, go_ref, glr_ref, wup_ref, bgk_ref, gn_ref, o_ref, s_scr, *, tb):
    @pl.when(pl.program_id(2) == 0)
    def _():
        s_scr[...] = jnp.zeros_like(s_scr)

    c = GLA_CHUNK
    x_hi, x_lo = _split_bf16(glr_ref[...])
    w_hi, w_lo = _split_bf16(wup_ref[...])
    gk = _dot(x_hi, w_hi) + _dot(x_hi, w_lo) + _dot(x_lo, w_hi) + bgk_ref[...]
    log_decay = _log_sigmoid(gk) / GLA_GATE_NORMALIZER
    ld_hi, ld_lo = _split_bf16(log_decay)

    ri = lax.broadcasted_iota(jnp.int32, (c, c), 0)
    ci = lax.broadcasted_iota(jnp.int32, (c, c), 1)
    causal = ri >= ci
    tri = jnp.where(causal, 1.0, 0.0).astype(BF16)
    gn = gn_ref[...]
    for n in range(tb // c):
        rows = slice(n * c, (n + 1) * c)
        b = _dot(tri, ld_hi[rows]) + _dot(tri, ld_lo[rows])
        b_last = b[c - 1:c]
        q = q_ref[rows, :].astype(F32) * (GLA_DK ** -0.5)
        k = k_ref[rows, :].astype(F32)
        v = v_ref[rows, :]
        q_dec = (q * jnp.exp(b)).astype(BF16)
        k_inv = (k * jnp.exp(-b)).astype(BF16)
        k_tail = (k * jnp.exp(b_last - b)).astype(BF16)
        a = lax.dot_general(q_dec, k_inv, NT_DIMS, preferred_element_type=F32)
        a = jnp.where(causal, a, 0.0).astype(BF16)
        s_in = s_scr[...]
        o = _dot(a, v) + lax.dot_general(q_dec, s_in.astype(BF16), NT_DIMS,
                                         preferred_element_type=F32)
        kv_t = lax.dot_general(v, k_tail, TN_DIMS, preferred_element_type=F32)
        s_scr[...] = s_in * jnp.exp(b_last) + kv_t
        go = go_ref[rows, :].astype(F32)
        o = _rms_scale(o, EPS) * gn * (go * _sigmoid(go))
        o_ref[rows, :] = o.astype(BF16)


def _gla(u, glr, wup_pad, b_gk, gla_norm_g, *, batch, seq, tb):
    m = u.shape[0]
    nt = seq // tb
    kernel = functools.partial(_gla_kernel, tb=tb)
    return pl.pallas_call(
        kernel,
        out_shape=jax.ShapeDtypeStruct((m, GLA_V_WIDTH), BF16),
        grid=(batch, GLA_HEADS, nt),
        in_specs=[
            pl.BlockSpec((tb, GLA_DK), lambda b, h, t: (b * nt + t, COL_GQ // GLA_DK + h)),
            pl.BlockSpec((tb, GLA_DK), lambda b, h, t: (b * nt + t, COL_GK // GLA_DK + h)),
            pl.BlockSpec((tb, GLA_DV), lambda b, h, t: (b * nt + t, COL_GV // GLA_DV + h)),
            pl.BlockSpec((tb, GLA_DV), lambda b, h, t: (b * nt + t, COL_GOUT // GLA_DV + h)),
            pl.BlockSpec((tb, LANES), lambda b, h, t: (b * nt + t, 0)),
            pl.BlockSpec((LANES, GLA_DK), lambda b, h, t: (0, h)),
            pl.BlockSpec((1, GLA_DK), lambda b, h, t: (0, h)),
            pl.BlockSpec((1, GLA_DV), lambda b, h, t: (0, 0)),
        ],
        out_specs=pl.BlockSpec((tb, GLA_DV), lambda b, h, t: (b * nt + t, h)),
        scratch_shapes=[pltpu.VMEM((GLA_DV, GLA_DK), F32)],
        compiler_params=_params(("parallel", "parallel", "arbitrary")),
        name="gla",
    )(u, u, u, u, glr, wup_pad, b_gk, gla_norm_g)


def _attn_kernel(lq1_ref, lk1_ref, lq2_ref, lk2_ref, sg_ref, q_ref, k_ref, v_ref, o_ref,
                 m_scr, l_scr, a_scr, acc_scr, s_scr, p_scr, *, blk, rc):
    qi = pl.program_id(2)
    d = DIFF_HEAD_DIM
    q = q_ref[...]
    qs = (q[:, :d], q[:, d:])

    m_scr[...] = jnp.full_like(m_scr, -jnp.inf)
    l_scr[...] = jnp.zeros_like(l_scr)
    acc_scr[...] = jnp.zeros_like(acc_scr)

    def scores(j, buf):
        start = pl.multiple_of(j * blk, blk)
        kblk = k_ref[pl.ds(start, blk), :]
        for sub in range(2):
            for n in range(blk // MXU_DIM):
                ksub = kblk[n * MXU_DIM:(n + 1) * MXU_DIM, sub * d:(sub + 1) * d]
                s_scr[buf, sub, :, n * MXU_DIM:(n + 1) * MXU_DIM] = lax.dot_general(
                    qs[sub], ksub, NT_DIMS, preferred_element_type=F32)

    def softmax_pv(j, buf, masked):
        start = pl.multiple_of(j * blk, blk)
        vblk = v_ref[pl.ds(start, blk), :]
        for sub in range(2):
            def score_cols(rows, row_base):
                cols = [s_scr[buf, sub, rows, c * LANES:(c + 1) * LANES] for c in range(blk // LANES)]
                if masked:
                    shape = cols[0].shape
                    ri = row_base + lax.broadcasted_iota(jnp.int32, shape, 0)
                    ci = lax.broadcasted_iota(jnp.int32, shape, 1)
                    cols = [jnp.where(ri >= ci + c * LANES, col, -jnp.inf) for c, col in enumerate(cols)]
                return cols

            m_old = m_scr[sub]
            row_max = jnp.max(functools.reduce(jnp.maximum, score_cols(slice(None), 0)),
                              axis=-1, keepdims=True)
            m_new = jnp.maximum(m_old, row_max)
            a_scr[buf, sub] = jnp.exp2(m_old - m_new)
            m_scr[sub] = m_new
            for r in range(blk // rc):
                rows = slice(r * rc, (r + 1) * rc)
                m_rows = m_scr[sub, rows, :]
                ps = [jnp.exp2(col - m_rows) for col in score_cols(rows, r * rc)]
                l_scr[sub, rows, :] = (a_scr[buf, sub, rows, :] * l_scr[sub, rows, :]
                                       + functools.reduce(jnp.add, ps))
                p_scr[buf, sub, rows, :] = jnp.concatenate(ps, axis=1).astype(BF16)
            for r in range(blk // MXU_DIM):
                rows = slice(r * MXU_DIM, (r + 1) * MXU_DIM)
                pv = _dot(p_scr[buf, sub, rows, :], vblk)
                alpha = a_scr[buf, sub, rows, :]
                acc_scr[sub, rows, :] = (jnp.concatenate([alpha, alpha], axis=1) * acc_scr[sub, rows, :]
                                         + pv)

    def pair(i, carry):
        j = 2 * i
        scores(j + 1, 1)
        softmax_pv(j, 0, masked=False)
        scores(j + 2, 0)
        softmax_pv(j + 1, 1, masked=False)
        return carry

    scores(0, 0)
    lax.fori_loop(0, qi // 2, pair, 0)

    @pl.when(qi % 2 == 1)
    def _():
        scores(qi, 1)
        softmax_pv(qi - 1, 0, masked=False)
        softmax_pv(qi, 1, masked=True)

    @pl.when(qi % 2 == 0)
    def _():
        softmax_pv(qi, 0, masked=True)

    lam = (jnp.exp(jnp.sum(lq1_ref[...] * lk1_ref[...], axis=-1, keepdims=True))
           - jnp.exp(jnp.sum(lq2_ref[...] * lk2_ref[...], axis=-1, keepdims=True))
           + LAMBDA_INIT)
    l1 = jnp.sum(l_scr[0], axis=-1, keepdims=True)
    l2 = jnp.sum(l_scr[1], axis=-1, keepdims=True)
    o = acc_scr[0] / l1 - lam * (acc_scr[1] / l2)
    o = _rms_scale(o, SUBLN_EPS) * sg_ref[...] * (1.0 - LAMBDA_INIT)
    o_ref[...] = o.astype(BF16)


def _diff_attn(u, lq1, lk1, lq2, lk2, subln_g, *, batch, seq, blk, rc):
    m = u.shape[0]
    tq = blk
    nq = seq // tq
    width = 2 * DIFF_HEAD_DIM
    vec = pl.BlockSpec((1, DIFF_HEAD_DIM), lambda b, h, i: (0, 0))
    kernel = functools.partial(_attn_kernel, blk=blk, rc=rc)
    return pl.pallas_call(
        kernel,
        out_shape=jax.ShapeDtypeStruct((m, DIFF_HEADS * DIFF_V_DIM), BF16),
        grid=(batch, DIFF_HEADS, nq),
        in_specs=[
            vec, vec, vec, vec,
            pl.BlockSpec((1, DIFF_V_DIM), lambda b, h, i: (0, 0)),
            pl.BlockSpec((tq, width), lambda b, h, i: (b * nq + i, COL_DQ // width + h)),
            pl.BlockSpec((seq, width), lambda b, h, i: (b, COL_DK // width + h)),
            pl.BlockSpec((seq, DIFF_V_DIM), lambda b, h, i: (b, COL_DV // DIFF_V_DIM + h)),
        ],
        out_specs=pl.BlockSpec((tq, DIFF_V_DIM), lambda b, h, i: (b * nq + i, h)),
        scratch_shapes=[pltpu.VMEM((2, blk, LANES), F32),
                        pltpu.VMEM((2, blk, LANES), F32),
                        pltpu.VMEM((2, 2, blk, LANES), F32),
                        pltpu.VMEM((2, blk, DIFF_V_DIM), F32),
                        pltpu.VMEM((2, 2, blk, blk), F32),
                        pltpu.VMEM((2, 2, blk, blk), BF16)],
        compiler_params=_params(("parallel", "parallel", "arbitrary")),
        name="diff_attn",
    )(lq1, lk1, lq2, lk2, subln_g, u, u, u)


def _branch_kernel(og_ref, od_ref, wg_ref, wd_ref, mg_ref, md_ref, o_ref, wg_scr, wd_scr):
    @pl.when(pl.program_id(1) == 0)
    def _():
        wg_scr[...] = wg_ref[...].astype(BF16)
        wd_scr[...] = wd_ref[...].astype(BF16)

    yg = _dot(og_ref[...], wg_scr[...])
    yd = _dot(od_ref[...], wd_scr[...])
    mixed = _sigmoid(mg_ref[...].astype(F32)) * yg + _sigmoid(md_ref[...].astype(F32)) * yd
    o_ref[...] = mixed.astype(BF16)


def _branch(o_gla, o_diff, w_branch, u, *, tm, tn):
    m = u.shape[0]
    nrow = GLA_V_WIDTH // D_MODEL
    return pl.pallas_call(
        _branch_kernel,
        out_shape=jax.ShapeDtypeStruct((m, D_MODEL), BF16),
        grid=(D_MODEL // tn, m // tm),
        in_specs=[
            pl.BlockSpec((tm, GLA_V_WIDTH), lambda j, i: (i, 0)),
            pl.BlockSpec((tm, D_MODEL), lambda j, i: (i, 0)),
            pl.BlockSpec((None, GLA_V_WIDTH, tn), lambda j, i: (0, 0, j)),
            pl.BlockSpec((None, D_MODEL, tn), lambda j, i: (0, nrow, j)),
            pl.BlockSpec((tm, tn), lambda j, i: (i, COL_MGLA // tn + j)),
            pl.BlockSpec((tm, tn), lambda j, i: (i, COL_MDIFF // tn + j)),
        ],
        out_specs=pl.BlockSpec((tm, tn), lambda j, i: (i, j)),
        scratch_shapes=[pltpu.VMEM((GLA_V_WIDTH, tn), BF16), pltpu.VMEM((D_MODEL, tn), BF16)],
        compiler_params=_params(("parallel", "arbitrary")),
        name="branch",
    )(o_gla, o_diff, w_branch, w_branch, u, u)


def _outproj_kernel(mixed_ref, x_ref, w_ref, g_ref, x1_ref, h2_ref):
    x1 = x_ref[...] + _dot(mixed_ref[...], w_ref[...])
    x1_ref[...] = x1
    h2_ref[...] = (_rms_scale(x1, EPS) * g_ref[...]).astype(BF16)


def _out_proj(mixed, x2d, w_out, g2, *, tm):
    m = x2d.shape[0]
    return pl.pallas_call(
        _outproj_kernel,
        out_shape=(jax.ShapeDtypeStruct((m, D_MODEL), F32),
                   jax.ShapeDtypeStruct((m, D_MODEL), BF16)),
        grid=(m // tm,),
        in_specs=[
            pl.BlockSpec((tm, D_MODEL), lambda i: (i, 0)),
            pl.BlockSpec((tm, D_MODEL), lambda i: (i, 0)),
            pl.BlockSpec((D_MODEL, D_MODEL), lambda i: (0, 0)),
            pl.BlockSpec((1, D_MODEL), lambda i: (0, 0)),
        ],
        out_specs=(pl.BlockSpec((tm, D_MODEL), lambda i: (i, 0)),
                   pl.BlockSpec((tm, D_MODEL), lambda i: (i, 0))),
        compiler_params=_params(("parallel",)),
        name="out_proj",
    )(mixed, x2d, w_out, g2)


def _ffn_up_kernel(h2_ref, wa_ref, wv_ref, cw_ref, cb_ref, act_ref, wa_scr, wv_scr, carry_scr, *,
                   tiles_per_seq):
    i = pl.program_id(1)

    @pl.when(i == 0)
    def _():
        wa_scr[...] = wa_ref[...].astype(BF16)
        wv_scr[...] = wv_ref[...].astype(BF16)

    @pl.when(i % tiles_per_seq == 0)
    def _():
        carry_scr[...] = jnp.zeros_like(carry_scr)

    h2 = h2_ref[...]
    a = _dot(h2, wa_scr[...])
    val = _dot(h2, wv_scr[...])
    tm = a.shape[0]
    carry = carry_scr[...]
    row = lax.broadcasted_iota(jnp.int32, carry.shape, 0)

    def shifted(k):
        r = pltpu.roll(a, k, 0)
        head = jnp.where(row < k, pltpu.roll(carry, k, 0), r[:SUBLANES])
        return jnp.concatenate([head, r[SUBLANES:]], axis=0)

    conv = cb_ref[...] + shifted(2) * cw_ref[0:1, :] + shifted(1) * cw_ref[1:2, :] + a * cw_ref[2:3, :]
    carry_scr[...] = a[tm - SUBLANES:]
    act_ref[...] = (conv * _sigmoid(conv) * val).astype(BF16)


def _ffn_up(h2, w_up, conv_w, conv_b, *, seq, tm, tf):
    m = h2.shape[0]
    nf = FFN_HIDDEN // tf
    kernel = functools.partial(_ffn_up_kernel, tiles_per_seq=seq // tm)
    return pl.pallas_call(
        kernel,
        out_shape=jax.ShapeDtypeStruct((m, FFN_HIDDEN), BF16),
        grid=(nf, m // tm),
        in_specs=[
            pl.BlockSpec((tm, D_MODEL), lambda f, i: (i, 0)),
            pl.BlockSpec((None, D_MODEL, tf), lambda f, i: (0, 0, f)),
            pl.BlockSpec((None, D_MODEL, tf), lambda f, i: (0, 0, nf + f)),
            pl.BlockSpec((None, CONV_WIDTH, tf), lambda f, i: (0, 0, f)),
            pl.BlockSpec((1, tf), lambda f, i: (0, f)),
        ],
        out_specs=pl.BlockSpec((tm, tf), lambda f, i: (i, f)),
        scratch_shapes=[pltpu.VMEM((D_MODEL, tf), BF16), pltpu.VMEM((D_MODEL, tf), BF16),
                        pltpu.VMEM((SUBLANES, tf), F32)],
        compiler_params=_params(("parallel", "arbitrary")),
        name="ffn_up",
    )(h2, w_up, w_up, conv_w, conv_b)


def _ffn_down_kernel(act_ref, w_ref, x1_ref, o_ref):
    @pl.when(pl.program_id(1) == 0)
    def _():
        o_ref[...] = x1_ref[...]

    o_ref[...] += _dot(act_ref[...], w_ref[...].astype(BF16))


def _ffn_down(act, w_down, x1, *, tm, tk):
    m = x1.shape[0]
    return pl.pallas_call(
        _ffn_down_kernel,
        out_shape=jax.ShapeDtypeStruct((m, D_MODEL), F32),
        grid=(m // tm, FFN_HIDDEN // tk),
        in_specs=[
            pl.BlockSpec((tm, tk), lambda i, k: (i, k)),
            pl.BlockSpec((None, tk, D_MODEL), lambda i, k: (0, k, 0)),
            pl.BlockSpec((tm, D_MODEL), lambda i, k: (i, 0)),
        ],
        out_specs=pl.BlockSpec((tm, D_MODEL), lambda i, k: (i, 0)),
        compiler_params=_params(("parallel", "arbitrary")),
        name="ffn_down",
    )(act, w_down, x1)


def kernel(x, positions, norm1_g, w_in, w_gk_up, b_gk, gla_norm_g, q_norm_g, k_norm_g, lambda_q1,
           lambda_k1, lambda_q2, lambda_k2, subln_g, w_branch, w_out, norm2_g, w_ffn_up, conv_w,
           conv_b, w_ffn_down):
    batch, seq, _ = x.shape
    m = batch * seq
    l = 0
    x2d = x.reshape(m, D_MODEL)
    pos2d = positions.reshape(m, 1)
    wup_pad = jnp.pad(w_gk_up[l], ((0, LANES - GLA_GATE_RANK), (0, 0)))
    qk_gain = jnp.stack([q_norm_g[l], k_norm_g[l]])[:, None, :]

    h, glr = _norm1(x2d, norm1_g[l][None, :], w_in, tm=512)
    cos_tab, sin_tab = _rope_tables(pos2d, tm=1024)
    u = _in_proj(h, w_in, cos_tab, sin_tab, qk_gain, tm=2048, tn=512, rc=256)
    o_gla = _gla(u, glr, wup_pad, b_gk[l][None, :], gla_norm_g[l][None, :], batch=batch, seq=seq, tb=512)
    o_diff = _diff_attn(u, lambda_q1[l][None, :], lambda_k1[l][None, :], lambda_q2[l][None, :],
                        lambda_k2[l][None, :], subln_g[l][None, :], batch=batch, seq=seq, blk=512, rc=64)
    mixed = _branch(o_gla, o_diff, w_branch, u, tm=1024, tn=512)
    x1, h2 = _out_proj(mixed, x2d, w_out[l].astype(BF16), norm2_g[l][None, :], tm=512)
    act = _ffn_up(h2, w_ffn_up, conv_w, conv_b[l][None, :], seq=seq, tm=1024, tf=512)
    out = _ffn_down(act, w_ffn_down, x1, tm=1024, tk=512)
    return out.reshape(batch, seq, D_MODEL)
```

```python
import functools
import math

import jax
import jax.numpy as jnp
from jax import lax
from jax.experimental import pallas as pl
from jax.experimental.pallas import tpu as pltpu

F32 = jnp.float32
BF16 = jnp.bfloat16

D_MODEL = 2048
GLA_HEADS = 4
GLA_DK = 256
GLA_DV = 512
GLA_K_WIDTH = GLA_HEADS * GLA_DK
GLA_V_WIDTH = GLA_HEADS * GLA_DV
GLA_GATE_RANK = 16
GLA_GATE_NORMALIZER = 16.0
GLA_CHUNK = 64
DIFF_HEAD_DIM = 128
DIFF_HEADS = 8
DIFF_V_DIM = 256
ROPE_THETA = 500000.0
ROPE_DIM = 32
FFN_HIDDEN = 5632
CONV_WIDTH = 3
EPS = 1e-6
SUBLN_EPS = 1e-5
LAMBDA_INIT = 0.8 - 0.6 * math.exp(-0.3 * 0)
LOG2_E = math.log2(math.e)

LR_START = 2 * GLA_K_WIDTH + GLA_V_WIDTH
LR_END = LR_START + GLA_GATE_RANK
PACKED_WIDTH = 16384
COL_GQ, COL_GK, COL_GV, COL_GOUT = 0, 1024, 2048, 4096
COL_DQ, COL_DK, COL_DV, COL_MGLA, COL_MDIFF = 6144, 8192, 10240, 12288, 14336

LANES = 128
SUBLANES = 8
MXU_DIM = 256
VMEM_LIMIT_BYTES = 56 * 1024 * 1024

NT_DIMS = (((1,), (1,)), ((), ()))
TN_DIMS = (((0,), (0,)), ((), ()))


def _params(semantics):
    return pltpu.CompilerParams(dimension_semantics=semantics, vmem_limit_bytes=VMEM_LIMIT_BYTES)


def _sigmoid(x):
    return 1.0 / (1.0 + jnp.exp(-x))


def _log_sigmoid(x):
    return jnp.minimum(x, 0.0) - jnp.log1p(jnp.exp(-jnp.abs(x)))


def _rms_scale(x, eps):
    return x * lax.rsqrt(jnp.mean(x * x, axis=-1, keepdims=True) + eps)


def _split_bf16(x):
    hi = x.astype(BF16)
    lo = (x - hi.astype(F32)).astype(BF16)
    return hi, lo


def _dot(a, b):
    return jnp.dot(a, b, preferred_element_type=F32)


def _dot_nt(a, b):
    return lax.dot_general(a, b, NT_DIMS, preferred_element_type=F32)


def _norm1_kernel(x_ref, g_ref, wlr_ref, h_ref, glr_ref):
    h = (_rms_scale(x_ref[...], EPS) * g_ref[...]).astype(BF16)
    h_ref[...] = h
    glr_ref[...] = _dot_nt(h, wlr_ref[...].astype(BF16))


def _norm1(x2d, g, w_in_t, *, tm):
    m = x2d.shape[0]
    return pl.pallas_call(
        _norm1_kernel,
        out_shape=(jax.ShapeDtypeStruct((m, D_MODEL), BF16),
                   jax.ShapeDtypeStruct((m, LANES), F32)),
        grid=(m // tm,),
        in_specs=[
            pl.BlockSpec((tm, D_MODEL), lambda i: (i, 0)),
            pl.BlockSpec((1, D_MODEL), lambda i: (0, 0)),
            pl.BlockSpec((None, LANES, D_MODEL), lambda i: (0, LR_START // LANES, 0)),
        ],
        out_specs=(pl.BlockSpec((tm, D_MODEL), lambda i: (i, 0)),
                   pl.BlockSpec((tm, LANES), lambda i: (i, 0))),
        compiler_params=_params(("parallel",)),
        name="norm1",
    )(x2d, g, w_in_t)


def _rope_kernel(pos_ref, freq_ref, sgn_ref, cos_ref, sin_ref):
    ang = pos_ref[...].astype(F32) * freq_ref[...]
    cos_ref[...] = jnp.cos(ang)
    sin_ref[...] = jnp.sin(ang) * sgn_ref[...]


def _rope_tables(pos2d, *, tm):
    m = pos2d.shape[0]
    inv_freq = ROPE_THETA ** (-jnp.arange(0, ROPE_DIM, 2, dtype=F32) / ROPE_DIM)
    pad = jnp.zeros((DIFF_HEAD_DIM - ROPE_DIM,), F32)
    freq = jnp.concatenate([inv_freq, inv_freq, pad])[None, :]
    half = jnp.ones((ROPE_DIM // 2,), F32)
    sgn = jnp.concatenate([-half, half, pad])[None, :]
    vec = pl.BlockSpec((1, DIFF_HEAD_DIM), lambda i: (0, 0))
    tab = pl.BlockSpec((tm, DIFF_HEAD_DIM), lambda i: (i, 0))
    return pl.pallas_call(
        _rope_kernel,
        out_shape=(jax.ShapeDtypeStruct((m, DIFF_HEAD_DIM), F32),) * 2,
        grid=(m // tm,),
        in_specs=[pl.BlockSpec((tm, 1), lambda i: (i, 0)), vec, vec],
        out_specs=(tab, tab),
        compiler_params=_params(("parallel",)),
        name="rope_tab",
    )(pos2d, freq, sgn)


def _inproj_kernel(h_ref, wt_ref, cos_ref, sin_ref, qkg_ref, u_ref, *, tn, rc):
    col0 = pl.program_id(1) * tn
    is_q = (col0 >= COL_DQ) & (col0 < COL_DK)
    is_k = (col0 >= COL_DK) & (col0 < COL_DV)

    @pl.when(jnp.logical_not(is_q | is_k))
    def _():
        u_ref[...] = _dot_nt(h_ref[...], wt_ref[0].astype(BF16)).astype(BF16)

    @pl.when(is_q | is_k)
    def _():
        w = wt_ref[0].astype(BF16)
        scale = jnp.where(is_q, LOG2_E * DIFF_HEAD_DIM ** -0.5, 1.0).astype(F32)
        g = jnp.where(is_q, qkg_ref[0], qkg_ref[1])
        first_half = lax.broadcasted_iota(jnp.int32, (rc, DIFF_HEAD_DIM), 1) < ROPE_DIM // 2
        for c in range(h_ref.shape[0] // rc):
            rows = slice(c * rc, (c + 1) * rc)
            acc = _dot_nt(h_ref[rows, :], w)
            cosf = cos_ref[rows, :] * scale
            sinf = sin_ref[rows, :] * scale
            for hh in range(tn // DIFF_HEAD_DIM):
                cols = slice(hh * DIFF_HEAD_DIM, (hh + 1) * DIFF_HEAD_DIM)
                y = _rms_scale(acc[:, cols], EPS) * g
                partner = jnp.where(first_half,
                                    pltpu.roll(y, DIFF_HEAD_DIM - ROPE_DIM // 2, 1),
                                    pltpu.roll(y, ROPE_DIM // 2, 1))
                u_ref[rows, cols] = (y * cosf + partner * sinf).astype(BF16)


def _in_proj(h, w_in_t, cos_tab, sin_tab, qk_gain, *, tm, tn, rc):
    m = h.shape[0]
    kernel = functools.partial(_inproj_kernel, tn=tn, rc=rc)
    tab = pl.BlockSpec((tm, DIFF_HEAD_DIM), lambda i, j: (i, 0))

    def wt_map(i, j):
        row = j * tn + jnp.where(j * tn >= LR_START, GLA_GATE_RANK, 0)
        return (0, pl.multiple_of(row, GLA_GATE_RANK), 0)

    return pl.pallas_call(
        kernel,
        out_shape=jax.ShapeDtypeStruct((m, PACKED_WIDTH), BF16),
        grid=(m // tm, PACKED_WIDTH // tn),
        in_specs=[
            pl.BlockSpec((tm, D_MODEL), lambda i, j: (i, 0)),
            pl.BlockSpec((pl.Element(1), pl.Element(tn), pl.Element(D_MODEL)), wt_map),
            tab, tab,
            pl.BlockSpec((2, 1, DIFF_HEAD_DIM), lambda i, j: (0, 0, 0)),
        ],
        out_specs=pl.BlockSpec((tm, tn), lambda i, j: (i, j)),
        compiler_params=_params(("parallel", "arbitrary")),
        name="in_proj",
    )(h, w_in_t, cos_tab, sin_tab, qk_gain)


def _gla_kernel(q_ref, k_ref, v_ref, go_ref, glr_ref, wup_ref, bgk_ref, gn_ref, o_ref, s_scr, *, tb):
    @pl.when(pl.program_id(2) == 0)
    def _():
        s_scr[...] = jnp.zeros_like(s_scr)

    c = GLA_CHUNK
    x_hi, x_lo = _split_bf16(glr_ref[...])
    w_hi, w_lo = _split_bf16(wup_ref[...])
    gk = _dot(x_hi, w_hi) + _dot(x_hi, w_lo) + _dot(x_lo, w_hi) + bgk_ref[...]
    log_decay = _log_sigmoid(gk) / GLA_GATE_NORMALIZER
    ld_hi, ld_lo = _split_bf16(log_decay)

    ri = lax.broadcasted_iota(jnp.int32, (c, c), 0)
    ci = lax.broadcasted_iota(jnp.int32, (c, c), 1)
    causal = ri >= ci
    tri = jnp.where(causal, 1.0, 0.0).astype(BF16)
    gn = gn_ref[...]
    for n in range(tb // c):
        rows = slice(n * c, (n + 1) * c)
        b = _dot(tri, ld_hi[rows]) + _dot(tri, ld_lo[rows])
        b_last = b[c - 1:c]
        q = q_ref[rows, :].astype(F32) * (GLA_DK ** -0.5)
        k = k_ref[rows, :].astype(F32)
        v = v_ref[rows, :]
        q_dec = (q * jnp.exp(b)).astype(BF16)
        k_inv = (k * jnp.exp(-b)).astype(BF16)
        k_tail = (k * jnp.exp(b_last - b)).astype(BF16)
        a = lax.dot_general(q_dec, k_inv, NT_DIMS, preferred_element_type=F32)
        a = jnp.where(causal, a, 0.0).astype(BF16)
        s_in = s_scr[...]
        o = _dot(a, v) + lax.dot_general(q_dec, s_in.astype(BF16), NT_DIMS,
                                         preferred_element_type=F32)
        kv_t = lax.dot_general(v, k_tail, TN_DIMS, preferred_element_type=F32)
        s_scr[...] = s_in * jnp.exp(b_last) + kv_t
        go = go_ref[rows, :].astype(F32)
        o = _rms_scale(o, EPS) * gn * (go * _sigmoid(go))
        o_ref[rows, :] = o.astype(BF16)


def _gla(u, glr, wup_pad, b_gk, gla_norm_g, *, batch, seq, tb):
    m = u.shape[0]
    nt = seq // tb
    kernel = functools.partial(_gla_kernel, tb=tb)
    return pl.pallas_call(
        kernel,
        out_shape=jax.ShapeDtypeStruct((m, GLA_V_WIDTH), BF16),
        grid=(batch, GLA_HEADS, nt),
        in_specs=[
            pl.BlockSpec((tb, GLA_DK), lambda b, h, t: (b * nt + t, COL_GQ // GLA_DK + h)),
            pl.BlockSpec((tb, GLA_DK), lambda b, h, t: (b * nt + t, COL_GK // GLA_DK + h)),
            pl.BlockSpec((tb, GLA_DV), lambda b, h, t: (b * nt + t, COL_GV // GLA_DV + h)),
            pl.BlockSpec((tb, GLA_DV), lambda b, h, t: (b * nt + t, COL_GOUT // GLA_DV + h)),
            pl.BlockSpec((tb, LANES), lambda b, h, t: (b * nt + t, 0)),
            pl.BlockSpec((LANES, GLA_DK), lambda b, h, t: (0, h)),
            pl.BlockSpec((1, GLA_DK), lambda b, h, t: (0, h)),
            pl.BlockSpec((1, GLA_DV), lambda b, h, t: (0, 0)),
        ],
        out_specs=pl.BlockSpec((tb, GLA_DV), lambda b, h, t: (b * nt + t, h)),
        scratch_shapes=[pltpu.VMEM((GLA_DV, GLA_DK), F32)],
        compiler_params=_params(("parallel", "parallel", "arbitrary")),
        name="gla",
    )(u, u, u, u, glr, wup_pad, b_gk, gla_norm_g)


def _attn_kernel(lq1_ref, lk1_ref, lq2_ref, lk2_ref, sg_ref, q_ref, k_ref, v_ref, o_ref,
                 m_scr, l_scr, a_scr, acc_scr, s_scr, p_scr, *, blk, rc):
    qi = pl.program_id(2)
    d = DIFF_HEAD_DIM
    q = q_ref[...]
    qs = (q[:, :d], q[:, d:])

    m_scr[...] = jnp.full_like(m_scr, -jnp.inf)
    l_scr[...] = jnp.zeros_like(l_scr)
    acc_scr[...] = jnp.zeros_like(acc_scr)

    def scores(j, buf):
        start = pl.multiple_of(j * blk, blk)
        kblk = k_ref[pl.ds(start, blk), :]
        for sub in range(2):
            for n in range(blk // MXU_DIM):
                ksub = kblk[n * MXU_DIM:(n + 1) * MXU_DIM, sub * d:(sub + 1) * d]
                s_scr[buf, sub, :, n * MXU_DIM:(n + 1) * MXU_DIM] = lax.dot_general(
                    qs[sub], ksub, NT_DIMS, preferred_element_type=F32)

    def softmax_pv(j, buf, masked):
        start = pl.multiple_of(j * blk, blk)
        vblk = v_ref[pl.ds(start, blk), :]
        for sub in range(2):
            def score_cols(rows, row_base):
                cols = [s_scr[buf, sub, rows, c * LANES:(c + 1) * LANES] for c in range(blk // LANES)]
                if masked:
                    shape = cols[0].shape
                    ri = row_base + lax.broadcasted_iota(jnp.int32, shape, 0)
                    ci = lax.broadcasted_iota(jnp.int32, shape, 1)
                    cols = [jnp.where(ri >= ci + c * LANES, col, -jnp.inf) for c, col in enumerate(cols)]
                return cols

            m_old = m_scr[sub]
            row_max = jnp.max(functools.reduce(jnp.maximum, score_cols(slice(None), 0)),
                              axis=-1, keepdims=True)
            m_new = jnp.maximum(m_old, row_max)
            a_scr[buf, sub] = jnp.exp2(m_old - m_new)
            m_scr[sub] = m_new
            for r in range(blk // rc):
                rows = slice(r * rc, (r + 1) * rc)
                m_rows = m_scr[sub, rows, :]
                ps = [jnp.exp2(col - m_rows) for col in score_cols(rows, r * rc)]
                l_scr[sub, rows, :] = (a_scr[buf, sub, rows, :] * l_scr[sub, rows, :]
                                       + functools.reduce(jnp.add, ps))
                p_scr[buf, sub, rows, :] = jnp.concatenate(ps, axis=1).astype(BF16)
            for r in range(blk // MXU_DIM):
                rows = slice(r * MXU_DIM, (r + 1) * MXU_DIM)
                pv = _dot(p_scr[buf, sub, rows, :], vblk)
                alpha = a_scr[buf, sub, rows, :]
                acc_scr[sub, rows, :] = (jnp.concatenate([alpha, alpha], axis=1) * acc_scr[sub, rows, :]
                                         + pv)

    def pair(i, carry):
        j = 2 * i
        scores(j + 1, 1)
        softmax_pv(j, 0, masked=False)
        scores(j + 2, 0)
        softmax_pv(j + 1, 1, masked=False)
        return carry

    scores(0, 0)
    lax.fori_loop(0, qi // 2, pair, 0)

    @pl.when(qi % 2 == 1)
    def _():
        scores(qi, 1)
        softmax_pv(qi - 1, 0, masked=False)
        softmax_pv(qi, 1, masked=True)

    @pl.when(qi % 2 == 0)
    def _():
        softmax_pv(qi, 0, masked=True)

    lam = (jnp.exp(jnp.sum(lq1_ref[...] * lk1_ref[...], axis=-1, keepdims=True))
           - jnp.exp(jnp.sum(lq2_ref[...] * lk2_ref[...], axis=-1, keepdims=True))
           + LAMBDA_INIT)
    l1 = jnp.sum(l_scr[0], axis=-1, keepdims=True)
    l2 = jnp.sum(l_scr[1], axis=-1, keepdims=True)
    o = acc_scr[0] / l1 - lam * (acc_scr[1] / l2)
    o = _rms_scale(o, SUBLN_EPS) * sg_ref[...] * (1.0 - LAMBDA_INIT)
    o_ref[...] = o.astype(BF16)


def _diff_attn(u, lq1, lk1, lq2, lk2, subln_g, *, batch, seq, blk, rc):
    m = u.shape[0]
    tq = blk
    nq = seq // tq
    width = 2 * DIFF_HEAD_DIM
    vec = pl.BlockSpec((1, DIFF_HEAD_DIM), lambda b, h, i: (0, 0))
    kernel = functools.partial(_attn_kernel, blk=blk, rc=rc)
    return pl.pallas_call(
        kernel,
        out_shape=jax.ShapeDtypeStruct((m, DIFF_HEADS * DIFF_V_DIM), BF16),
        grid=(batch, DIFF_HEADS, nq),
        in_specs=[
            vec, vec, vec, vec,
            pl.BlockSpec((1, DIFF_V_DIM), lambda b, h, i: (0, 0)),
            pl.BlockSpec((tq, width), lambda b, h, i: (b * nq + i, COL_DQ // width + h)),
            pl.BlockSpec((seq, width), lambda b, h, i: (b, COL_DK // width + h)),
            pl.BlockSpec((seq, DIFF_V_DIM), lambda b, h, i: (b, COL_DV // DIFF_V_DIM + h)),
        ],
        out_specs=pl.BlockSpec((tq, DIFF_V_DIM), lambda b, h, i: (b * nq + i, h)),
        scratch_shapes=[pltpu.VMEM((2, blk, LANES), F32),
                        pltpu.VMEM((2, blk, LANES), F32),
                        pltpu.VMEM((2, 2, blk, LANES), F32),
                        pltpu.VMEM((2, blk, DIFF_V_DIM), F32),
                        pltpu.VMEM((2, 2, blk, blk), F32),
                        pltpu.VMEM((2, 2, blk, blk), BF16)],
        compiler_params=_params(("parallel", "parallel", "arbitrary")),
        name="diff_attn",
    )(lq1, lk1, lq2, lk2, subln_g, u, u, u)


def _branch_kernel(og_ref, od_ref, wg_ref, wd_ref, mg_ref, md_ref, o_ref, wg_scr, wd_scr):
    @pl.when(pl.program_id(1) == 0)
    def _():
        wg_scr[...] = wg_ref[...].astype(BF16)
        wd_scr[...] = wd_ref[...].astype(BF16)

    yg = _dot(og_ref[...], wg_scr[...])
    yd = _dot(od_ref[...], wd_scr[...])
    mixed = _sigmoid(mg_ref[...].astype(F32)) * yg + _sigmoid(md_ref[...].astype(F32)) * yd
    o_ref[...] = mixed.astype(BF16)


def _branch(o_gla, o_diff, w_branch, u, *, tm, tn):
    m = u.shape[0]
    nrow = GLA_V_WIDTH // D_MODEL
    return pl.pallas_call(
        _branch_kernel,
        out_shape=jax.ShapeDtypeStruct((m, D_MODEL), BF16),
        grid=(D_MODEL // tn, m // tm),
        in_specs=[
            pl.BlockSpec((tm, GLA_V_WIDTH), lambda j, i: (i, 0)),
            pl.BlockSpec((tm, D_MODEL), lambda j, i: (i, 0)),
            pl.BlockSpec((None, GLA_V_WIDTH, tn), lambda j, i: (0, 0, j)),
            pl.BlockSpec((None, D_MODEL, tn), lambda j, i: (0, nrow, j)),
            pl.BlockSpec((tm, tn), lambda j, i: (i, COL_MGLA // tn + j)),
            pl.BlockSpec((tm, tn), lambda j, i: (i, COL_MDIFF // tn + j)),
        ],
        out_specs=pl.BlockSpec((tm, tn), lambda j, i: (i, j)),
        scratch_shapes=[pltpu.VMEM((GLA_V_WIDTH, tn), BF16), pltpu.VMEM((D_MODEL, tn), BF16)],
        compiler_params=_params(("parallel", "arbitrary")),
        name="branch",
    )(o_gla, o_diff, w_branch, w_branch, u, u)


def _outproj_kernel(mixed_ref, x_ref, w_ref, g_ref, x1_ref, h2_ref):
    x1 = x_ref[...] + _dot(mixed_ref[...], w_ref[...])
    x1_ref[...] = x1
    h2_ref[...] = (_rms_scale(x1, EPS) * g_ref[...]).astype(BF16)


def _out_proj(mixed, x2d, w_out, g2, *, tm):
    m = x2d.shape[0]
    return pl.pallas_call(
        _outproj_kernel,
        out_shape=(jax.ShapeDtypeStruct((m, D_MODEL), F32),
                   jax.ShapeDtypeStruct((m, D_MODEL), BF16)),
        grid=(m // tm,),
        in_specs=[
            pl.BlockSpec((tm, D_MODEL), lambda i: (i, 0)),
            pl.BlockSpec((tm, D_MODEL), lambda i: (i, 0)),
            pl.BlockSpec((D_MODEL, D_MODEL), lambda i: (0, 0)),
            pl.BlockSpec((1, D_MODEL), lambda i: (0, 0)),
        ],
        out_specs=(pl.BlockSpec((tm, D_MODEL), lambda i: (i, 0)),
                   pl.BlockSpec((tm, D_MODEL), lambda i: (i, 0))),
        compiler_params=_params(("parallel",)),
        name="out_proj",
    )(mixed, x2d, w_out, g2)


def _ffn_up_kernel(h2_ref, wa_ref, wv_ref, cw_ref, cb_ref, act_ref, wa_scr, wv_scr, carry_scr, *,
                   tiles_per_seq):
    i = pl.program_id(1)

    @pl.when(i == 0)
    def _():
        wa_scr[...] = wa_ref[...].astype(BF16)
        wv_scr[...] = wv_ref[...].astype(BF16)

    @pl.when(i % tiles_per_seq == 0)
    def _():
        carry_scr[...] = jnp.zeros_like(carry_scr)

    h2 = h2_ref[...]
    a = _dot(h2, wa_scr[...])
    val = _dot(h2, wv_scr[...])
    tm = a.shape[0]
    carry = carry_scr[...]
    row = lax.broadcasted_iota(jnp.int32, carry.shape, 0)

    def shifted(k):
        r = pltpu.roll(a, k, 0)
        head = jnp.where(row < k, pltpu.roll(carry, k, 0), r[:SUBLANES])
        return jnp.concatenate([head, r[SUBLANES:]], axis=0)

    conv = cb_ref[...] + shifted(2) * cw_ref[0:1, :] + shifted(1) * cw_ref[1:2, :] + a * cw_ref[2:3, :]
    carry_scr[...] = a[tm - SUBLANES:]
    act_ref[...] = (conv * _sigmoid(conv) * val).astype(BF16)


def _ffn_up(h2, w_up, conv_w, conv_b, *, seq, tm, tf):
    m = h2.shape[0]
    nf = FFN_HIDDEN // tf
    kernel = functools.partial(_ffn_up_kernel, tiles_per_seq=seq // tm)
    return pl.pallas_call(
        kernel,
        out_shape=jax.ShapeDtypeStruct((m, FFN_HIDDEN), BF16),
        grid=(nf, m // tm),
        in_specs=[
            pl.BlockSpec((tm, D_MODEL), lambda f, i: (i, 0)),
            pl.BlockSpec((None, D_MODEL, tf), lambda f, i: (0, 0, f)),
            pl.BlockSpec((None, D_MODEL, tf), lambda f, i: (0, 0, nf + f)),
            pl.BlockSpec((None, CONV_WIDTH, tf), lambda f, i: (0, 0, f)),
            pl.BlockSpec((1, tf), lambda f, i: (0, f)),
        ],
        out_specs=pl.BlockSpec((tm, tf), lambda f, i: (i, f)),
        scratch_shapes=[pltpu.VMEM((D_MODEL, tf), BF16), pltpu.VMEM((D_MODEL, tf), BF16),
                        pltpu.VMEM((SUBLANES, tf), F32)],
        compiler_params=_params(("parallel", "arbitrary")),
        name="ffn_up",
    )(h2, w_up, w_up, conv_w, conv_b)


def _ffn_down_kernel(act_ref, w_ref, x1_ref, o_ref):
    @pl.when(pl.program_id(1) == 0)
    def _():
        o_ref[...] = x1_ref[...]

    o_ref[...] += _dot(act_ref[...], w_ref[...].astype(BF16))


def _ffn_down(act, w_down, x1, *, tm, tk):
    m = x1.shape[0]
    return pl.pallas_call(
        _ffn_down_kernel,
        out_shape=jax.ShapeDtypeStruct((m, D_MODEL), F32),
        grid=(m // tm, FFN_HIDDEN // tk),
        in_specs=[
            pl.BlockSpec((tm, tk), lambda i, k: (i, k)),
            pl.BlockSpec((None, tk, D_MODEL), lambda i, k: (0, k, 0)),
            pl.BlockSpec((tm, D_MODEL), lambda i, k: (i, 0)),
        ],
        out_specs=pl.BlockSpec((tm, D_MODEL), lambda i, k: (i, 0)),
        compiler_params=_params(("parallel", "arbitrary")),
        name="ffn_down",
    )(act, w_down, x1)


def kernel(x, positions, norm1_g, w_in, w_gk_up, b_gk, gla_norm_g, q_norm_g, k_norm_g, lambda_q1,
           lambda_k1, lambda_q2, lambda_k2, subln_g, w_branch, w_out, norm2_g, w_ffn_up, conv_w,
           conv_b, w_ffn_down):
    batch, seq, _ = x.shape
    m = batch * seq
    l = 0
    x2d = x.reshape(m, D_MODEL)
    pos2d = positions.reshape(m, 1)
    wup_pad = jnp.pad(w_gk_up[l], ((0, LANES - GLA_GATE_RANK), (0, 0)))
    qk_gain = jnp.stack([q_norm_g[l], k_norm_g[l]])[:, None, :]

    w_in_t = jnp.swapaxes(w_in, 1, 2)

    h, glr = _norm1(x2d, norm1_g[l][None, :], w_in_t, tm=512)
    cos_tab, sin_tab = _rope_tables(pos2d, tm=1024)
    u = _in_proj(h, w_in_t, cos_tab, sin_tab, qk_gain, tm=2048, tn=512, rc=256)
    o_gla = _gla(u, glr, wup_pad, b_gk[l][None, :], gla_norm_g[l][None, :], batch=batch, seq=seq, tb=512)
    o_diff = _diff_attn(u, lambda_q1[l][None, :], lambda_k1[l][None, :], lambda_q2[l][None, :],
                        lambda_k2[l][None, :], subln_g[l][None, :], batch=batch, seq=seq, blk=512, rc=64)
    mixed = _branch(o_gla, o_diff, w_branch, u, tm=1024, tn=512)
    x1, h2 = _out_proj(mixed, x2d, w_out[l].astype(BF16), norm2_g[l][None, :], tm=512)
    act = _ffn_up(h2, w_ffn_up, conv_w, conv_b[l][None, :], seq=seq, tm=1024, tf=512)
    out = _ffn_down(act, w_ffn_down, x1, tm=1024, tk=512)
    return out.reshape(batch, seq, D_MODEL)
```

```python
import functools
import math

import jax
import jax.numpy as jnp
from jax import lax
from jax.experimental import pallas as pl
from jax.experimental.pallas import tpu as pltpu

F32 = jnp.float32
BF16 = jnp.bfloat16

D_MODEL = 2048
GLA_HEADS = 4
GLA_DK = 256
GLA_DV = 512
GLA_K_WIDTH = GLA_HEADS * GLA_DK
GLA_V_WIDTH = GLA_HEADS * GLA_DV
GLA_GATE_RANK = 16
GLA_GATE_NORMALIZER = 16.0
GLA_CHUNK = 64
DIFF_HEAD_DIM = 128
DIFF_HEADS = 8
DIFF_V_DIM = 256
ROPE_THETA = 500000.0
ROPE_DIM = 32
FFN_HIDDEN = 5632
CONV_WIDTH = 3
EPS = 1e-6
SUBLN_EPS = 1e-5
LAMBDA_INIT = 0.8 - 0.6 * math.exp(-0.3 * 0)
LOG2_E = math.log2(math.e)

LR_START = 2 * GLA_K_WIDTH + GLA_V_WIDTH
LR_END = LR_START + GLA_GATE_RANK
PACKED_WIDTH = 16384
COL_GQ, COL_GK, COL_GV, COL_GOUT = 0, 1024, 2048, 4096
COL_DQ, COL_DK, COL_DV, COL_MGLA, COL_MDIFF = 6144, 8192, 10240, 12288, 14336

LANES = 128
SUBLANES = 8
MXU_DIM = 256
VMEM_LIMIT_BYTES = 56 * 1024 * 1024

NT_DIMS = (((1,), (1,)), ((), ()))
TN_DIMS = (((0,), (0,)), ((), ()))


def _params(semantics):
    return pltpu.CompilerParams(dimension_semantics=semantics, vmem_limit_bytes=VMEM_LIMIT_BYTES)


def _sigmoid(x):
    return 1.0 / (1.0 + jnp.exp(-x))


def _log_sigmoid(x):
    return jnp.minimum(x, 0.0) - jnp.log(1.0 + jnp.exp(-jnp.abs(x)))


def _rms_scale(x, eps):
    return x * lax.rsqrt(jnp.mean(x * x, axis=-1, keepdims=True) + eps)


def _split_bf16(x):
    hi = x.astype(BF16)
    lo = (x - hi.astype(F32)).astype(BF16)
    return hi, lo


def _dot(a, b):
    return jnp.dot(a, b, preferred_element_type=F32)


def _dot_nt(a, b):
    return lax.dot_general(a, b, NT_DIMS, preferred_element_type=F32)


def _norm1_kernel(x_ref, g_ref, wlr_ref, h_ref, glr_ref):
    h = (_rms_scale(x_ref[...], EPS) * g_ref[...]).astype(BF16)
    h_ref[...] = h
    glr_ref[...] = _dot_nt(h, wlr_ref[...].astype(BF16))


def _norm1(x2d, g, w_in_t, *, tm):
    m = x2d.shape[0]
    return pl.pallas_call(
        _norm1_kernel,
        out_shape=(jax.ShapeDtypeStruct((m, D_MODEL), BF16),
                   jax.ShapeDtypeStruct((m, LANES), F32)),
        grid=(m // tm,),
        in_specs=[
            pl.BlockSpec((tm, D_MODEL), lambda i: (i, 0)),
            pl.BlockSpec((1, D_MODEL), lambda i: (0, 0)),
            pl.BlockSpec((None, LANES, D_MODEL), lambda i: (0, LR_START // LANES, 0)),
        ],
        out_specs=(pl.BlockSpec((tm, D_MODEL), lambda i: (i, 0)),
                   pl.BlockSpec((tm, LANES), lambda i: (i, 0))),
        compiler_params=_params(("parallel",)),
        name="norm1",
    )(x2d, g, w_in_t)


def _rope_kernel(pos_ref, freq_ref, sgn_ref, cos_ref, sin_ref):
    ang = pos_ref[...].astype(F32) * freq_ref[...]
    cos_ref[...] = jnp.cos(ang)
    sin_ref[...] = jnp.sin(ang) * sgn_ref[...]


def _rope_tables(pos2d, *, tm):
    m = pos2d.shape[0]
    inv_freq = ROPE_THETA ** (-jnp.arange(0, ROPE_DIM, 2, dtype=F32) / ROPE_DIM)
    pad = jnp.zeros((DIFF_HEAD_DIM - ROPE_DIM,), F32)
    freq = jnp.concatenate([inv_freq, inv_freq, pad])[None, :]
    half = jnp.ones((ROPE_DIM // 2,), F32)
    sgn = jnp.concatenate([-half, half, pad])[None, :]
    vec = pl.BlockSpec((1, DIFF_HEAD_DIM), lambda i: (0, 0))
    tab = pl.BlockSpec((tm, DIFF_HEAD_DIM), lambda i: (i, 0))
    return pl.pallas_call(
        _rope_kernel,
        out_shape=(jax.ShapeDtypeStruct((m, DIFF_HEAD_DIM), F32),) * 2,
        grid=(m // tm,),
        in_specs=[pl.BlockSpec((tm, 1), lambda i: (i, 0)), vec, vec],
        out_specs=(tab, tab),
        compiler_params=_params(("parallel",)),
        name="rope_tab",
    )(pos2d, freq, sgn)


def _inproj_kernel(h_ref, wt_ref, cos_ref, sin_ref, qkg_ref, u_ref, *, tn, rc):
    col0 = pl.program_id(1) * tn
    is_q = (col0 >= COL_DQ) & (col0 < COL_DK)
    is_k = (col0 >= COL_DK) & (col0 < COL_DV)

    @pl.when(jnp.logical_not(is_q | is_k))
    def _():
        u_ref[...] = _dot_nt(h_ref[...], wt_ref[0].astype(BF16)).astype(BF16)

    @pl.when(is_q | is_k)
    def _():
        w = wt_ref[0].astype(BF16)
        scale = jnp.where(is_q, LOG2_E * DIFF_HEAD_DIM ** -0.5, 1.0).astype(F32)
        g = jnp.where(is_q, qkg_ref[0], qkg_ref[1])
        first_half = lax.broadcasted_iota(jnp.int32, (rc, DIFF_HEAD_DIM), 1) < ROPE_DIM // 2
        for c in range(h_ref.shape[0] // rc):
            rows = slice(c * rc, (c + 1) * rc)
            acc = _dot_nt(h_ref[rows, :], w)
            cosf = cos_ref[rows, :] * scale
            sinf = sin_ref[rows, :] * scale
            for hh in range(tn // DIFF_HEAD_DIM):
                cols = slice(hh * DIFF_HEAD_DIM, (hh + 1) * DIFF_HEAD_DIM)
                y = _rms_scale(acc[:, cols], EPS) * g
                partner = jnp.where(first_half,
                                    pltpu.roll(y, DIFF_HEAD_DIM - ROPE_DIM // 2, 1),
                                    pltpu.roll(y, ROPE_DIM // 2, 1))
                u_ref[rows, cols] = (y * cosf + partner * sinf).astype(BF16)


def _in_proj(h, w_in_t, cos_tab, sin_tab, qk_gain, *, tm, tn, rc):
    m = h.shape[0]
    kernel = functools.partial(_inproj_kernel, tn=tn, rc=rc)
    tab = pl.BlockSpec((tm, DIFF_HEAD_DIM), lambda i, j: (i, 0))

    def wt_map(i, j):
        row = j * tn + jnp.where(j * tn >= LR_START, GLA_GATE_RANK, 0)
        return (0, pl.multiple_of(row, GLA_GATE_RANK), 0)

    return pl.pallas_call(
        kernel,
        out_shape=jax.ShapeDtypeStruct((m, PACKED_WIDTH), BF16),
        grid=(m // tm, PACKED_WIDTH // tn),
        in_specs=[
            pl.BlockSpec((tm, D_MODEL), lambda i, j: (i, 0)),
            pl.BlockSpec((pl.Element(1), pl.Element(tn), pl.Element(D_MODEL)), wt_map),
            tab, tab,
            pl.BlockSpec((2, 1, DIFF_HEAD_DIM), lambda i, j: (0, 0, 0)),
        ],
        out_specs=pl.BlockSpec((tm, tn), lambda i, j: (i, j)),
        compiler_params=_params(("parallel", "arbitrary")),
        name="in_proj",
    )(h, w_in_t, cos_tab, sin_tab, qk_gain)


def _gla_kernel(q_ref, k_ref, v_ref, go_ref, glr_ref, wup_ref, bgk_ref, gn_ref, o_ref, s_scr, *, tb):
    @pl.when(pl.program_id(1) == 0)
    def _():
        s_scr[...] = jnp.zeros_like(s_scr)

    c = GLA_CHUNK
    gk = _dot(glr_ref[...].astype(BF16), wup_ref[...].astype(BF16)) + bgk_ref[...]
    log_decay = _log_sigmoid(gk) / GLA_GATE_NORMALIZER
    ld_hi, ld_lo = _split_bf16(log_decay)

    causal = lax.broadcasted_iota(jnp.int32, (c, c), 0) >= lax.broadcasted_iota(jnp.int32, (c, c), 1)
    tri = jnp.where(causal, 1.0, 0.0).astype(BF16)
    gn = gn_ref[...]
    for n in range(tb // c):
        rows = slice(n * c, (n + 1) * c)
        b = _dot(tri, ld_hi[rows]) + _dot(tri, ld_lo[rows])
        b_last = b[c - 1:c]
        q = q_ref[rows, :].astype(F32) * (GLA_DK ** -0.5)
        k = k_ref[rows, :].astype(F32)
        q_dec = (q * jnp.exp(b)).astype(BF16)
        k_inv = (k * jnp.exp(-b)).astype(BF16)
        k_tail = (k * jnp.exp(b_last - b)).astype(BF16)
        decay = jnp.exp(b_last)
        for h in range(GLA_HEADS):
            kc = slice(h * GLA_DK, (h + 1) * GLA_DK)
            vc = slice(h * GLA_DV, (h + 1) * GLA_DV)
            v = v_ref[rows, vc]
            a = jnp.where(causal, _dot_nt(q_dec[:, kc], k_inv[:, kc]), 0.0).astype(BF16)
            s_in = s_scr[h]
            o = _dot(a, v) + _dot_nt(q_dec[:, kc], s_in.astype(BF16))
            kv_t = lax.dot_general(v, k_tail[:, kc], TN_DIMS, preferred_element_type=F32)
            s_scr[h] = s_in * decay[:, kc] + kv_t
            go = go_ref[rows, vc].astype(F32)
            o_ref[rows, vc] = (_rms_scale(o, EPS) * gn * (go * _sigmoid(go))).astype(BF16)


def _gla(u, glr, wup_pad, b_gk, gla_norm_g, *, batch, seq, tb):
    m = u.shape[0]
    nt = seq // tb
    kernel = functools.partial(_gla_kernel, tb=tb)
    return pl.pallas_call(
        kernel,
        out_shape=jax.ShapeDtypeStruct((m, GLA_V_WIDTH), BF16),
        grid=(batch, nt),
        in_specs=[
            pl.BlockSpec((tb, GLA_K_WIDTH), lambda b, t: (b * nt + t, COL_GQ // GLA_K_WIDTH)),
            pl.BlockSpec((tb, GLA_K_WIDTH), lambda b, t: (b * nt + t, COL_GK // GLA_K_WIDTH)),
            pl.BlockSpec((tb, GLA_V_WIDTH), lambda b, t: (b * nt + t, COL_GV // GLA_V_WIDTH)),
            pl.BlockSpec((tb, GLA_V_WIDTH), lambda b, t: (b * nt + t, COL_GOUT // GLA_V_WIDTH)),
            pl.BlockSpec((tb, LANES), lambda b, t: (b * nt + t, 0)),
            pl.BlockSpec((LANES, GLA_K_WIDTH), lambda b, t: (0, 0)),
            pl.BlockSpec((1, GLA_K_WIDTH), lambda b, t: (0, 0)),
            pl.BlockSpec((1, GLA_DV), lambda b, t: (0, 0)),
        ],
        out_specs=pl.BlockSpec((tb, GLA_V_WIDTH), lambda b, t: (b * nt + t, 0)),
        scratch_shapes=[pltpu.VMEM((GLA_HEADS, GLA_DV, GLA_DK), F32)],
        compiler_params=_params(("parallel", "arbitrary")),
        name="gla",
    )(u, u, u, u, glr, wup_pad, b_gk, gla_norm_g)


def _attn_kernel(lq1_ref, lk1_ref, lq2_ref, lk2_ref, sg_ref, q_ref, k_ref, v_ref, o_ref,
                 m_scr, l_scr, a_scr, acc_scr, s_scr, p_scr, *, blk, rc):
    qi = pl.program_id(2)
    d = DIFF_HEAD_DIM
    q = q_ref[...]
    qs = (q[:, :d], q[:, d:])

    m_scr[...] = jnp.full_like(m_scr, -jnp.inf)
    l_scr[...] = jnp.zeros_like(l_scr)
    acc_scr[...] = jnp.zeros_like(acc_scr)

    def scores(j, buf):
        start = pl.multiple_of(j * blk, blk)
        kblk = k_ref[pl.ds(start, blk), :]
        for sub in range(2):
            for n in range(blk // MXU_DIM):
                ksub = kblk[n * MXU_DIM:(n + 1) * MXU_DIM, sub * d:(sub + 1) * d]
                s_scr[buf, sub, :, n * MXU_DIM:(n + 1) * MXU_DIM] = lax.dot_general(
                    qs[sub], ksub, NT_DIMS, preferred_element_type=F32)

    def softmax_pv(j, buf, masked):
        start = pl.multiple_of(j * blk, blk)
        vblk = v_ref[pl.ds(start, blk), :]
        for sub in range(2):
            def score_cols(rows, row_base):
                cols = [s_scr[buf, sub, rows, c * LANES:(c + 1) * LANES] for c in range(blk // LANES)]
                if masked:
                    shape = cols[0].shape
                    ri = row_base + lax.broadcasted_iota(jnp.int32, shape, 0)
                    ci = lax.broadcasted_iota(jnp.int32, shape, 1)
                    cols = [jnp.where(ri >= ci + c * LANES, col, -jnp.inf) for c, col in enumerate(cols)]
                return cols

            m_old = m_scr[sub]
            row_max = jnp.max(functools.reduce(jnp.maximum, score_cols(slice(None), 0)),
                              axis=-1, keepdims=True)
            m_new = jnp.maximum(m_old, row_max)
            a_scr[buf, sub] = jnp.exp2(m_old - m_new)
            m_scr[sub] = m_new
            for r in range(blk // rc):
                rows = slice(r * rc, (r + 1) * rc)
                m_rows = m_scr[sub, rows, :]
                ps = [jnp.exp2(col - m_rows) for col in score_cols(rows, r * rc)]
                l_scr[sub, rows, :] = (a_scr[buf, sub, rows, :] * l_scr[sub, rows, :]
                                       + functools.reduce(jnp.add, ps))
                p_scr[buf, sub, rows, :] = jnp.concatenate(ps, axis=1).astype(BF16)
            pv = functools.reduce(jnp.add, [
                _dot(p_scr[buf, sub, :, n * MXU_DIM:(n + 1) * MXU_DIM], vblk[n * MXU_DIM:(n + 1) * MXU_DIM, :])
                for n in range(blk // MXU_DIM)])
            alpha = a_scr[buf, sub]
            acc_scr[sub] = jnp.concatenate([alpha, alpha], axis=1) * acc_scr[sub] + pv

    def pair(i, carry):
        j = 2 * i
        scores(j + 1, 1)
        softmax_pv(j, 0, masked=False)
        scores(j + 2, 0)
        softmax_pv(j + 1, 1, masked=False)
        return carry

    scores(0, 0)
    lax.fori_loop(0, qi // 2, pair, 0)

    @pl.when(qi % 2 == 1)
    def _():
        scores(qi, 1)
        softmax_pv(qi - 1, 0, masked=False)
        softmax_pv(qi, 1, masked=True)

    @pl.when(qi % 2 == 0)
    def _():
        softmax_pv(qi, 0, masked=True)

    lam = (jnp.exp(jnp.sum(lq1_ref[...] * lk1_ref[...], axis=-1, keepdims=True))
           - jnp.exp(jnp.sum(lq2_ref[...] * lk2_ref[...], axis=-1, keepdims=True))
           + LAMBDA_INIT)
    l1 = jnp.sum(l_scr[0], axis=-1, keepdims=True)
    l2 = jnp.sum(l_scr[1], axis=-1, keepdims=True)
    o = acc_scr[0] / l1 - lam * (acc_scr[1] / l2)
    o = _rms_scale(o, SUBLN_EPS) * sg_ref[...] * (1.0 - LAMBDA_INIT)
    o_ref[...] = o.astype(BF16)


def _diff_attn(u, lq1, lk1, lq2, lk2, subln_g, *, batch, seq, blk, rc):
    m = u.shape[0]
    tq = blk
    nq = seq // tq
    width = 2 * DIFF_HEAD_DIM
    vec = pl.BlockSpec((1, DIFF_HEAD_DIM), lambda b, h, i: (0, 0))
    kernel = functools.partial(_attn_kernel, blk=blk, rc=rc)
    return pl.pallas_call(
        kernel,
        out_shape=jax.ShapeDtypeStruct((m, DIFF_HEADS * DIFF_V_DIM), BF16),
        grid=(batch, DIFF_HEADS, nq),
        in_specs=[
            vec, vec, vec, vec,
            pl.BlockSpec((1, DIFF_V_DIM), lambda b, h, i: (0, 0)),
            pl.BlockSpec((tq, width), lambda b, h, i: (b * nq + i, COL_DQ // width + h)),
            pl.BlockSpec((seq, width), lambda b, h, i: (b, COL_DK // width + h)),
            pl.BlockSpec((seq, DIFF_V_DIM), lambda b, h, i: (b, COL_DV // DIFF_V_DIM + h)),
        ],
        out_specs=pl.BlockSpec((tq, DIFF_V_DIM), lambda b, h, i: (b * nq + i, h)),
        scratch_shapes=[pltpu.VMEM((2, blk, LANES), F32),
                        pltpu.VMEM((2, blk, LANES), F32),
                        pltpu.VMEM((2, 2, blk, LANES), F32),
                        pltpu.VMEM((2, blk, DIFF_V_DIM), F32),
                        pltpu.VMEM((2, 2, blk, blk), F32),
                        pltpu.VMEM((2, 2, blk, blk), BF16)],
        compiler_params=_params(("parallel", "parallel", "arbitrary")),
        name="diff_attn",
    )(lq1, lk1, lq2, lk2, subln_g, u, u, u)


def _branch_kernel(og_ref, od_ref, wg_ref, wd_ref, mg_ref, md_ref, o_ref, wg_scr, wd_scr):
    @pl.when(pl.program_id(1) == 0)
    def _():
        wg_scr[...] = wg_ref[...].astype(BF16)
        wd_scr[...] = wd_ref[...].astype(BF16)

    yg = _dot(og_ref[...], wg_scr[...])
    yd = _dot(od_ref[...], wd_scr[...])
    mixed = _sigmoid(mg_ref[...].astype(F32)) * yg + _sigmoid(md_ref[...].astype(F32)) * yd
    o_ref[...] = mixed.astype(BF16)


def _branch(o_gla, o_diff, w_branch, u, *, tm, tn):
    m = u.shape[0]
    nrow = GLA_V_WIDTH // D_MODEL
    return pl.pallas_call(
        _branch_kernel,
        out_shape=jax.ShapeDtypeStruct((m, D_MODEL), BF16),
        grid=(D_MODEL // tn, m // tm),
        in_specs=[
            pl.BlockSpec((tm, GLA_V_WIDTH), lambda j, i: (i, 0)),
            pl.BlockSpec((tm, D_MODEL), lambda j, i: (i, 0)),
            pl.BlockSpec((None, GLA_V_WIDTH, tn), lambda j, i: (0, 0, j)),
            pl.BlockSpec((None, D_MODEL, tn), lambda j, i: (0, nrow, j)),
            pl.BlockSpec((tm, tn), lambda j, i: (i, COL_MGLA // tn + j)),
            pl.BlockSpec((tm, tn), lambda j, i: (i, COL_MDIFF // tn + j)),
        ],
        out_specs=pl.BlockSpec((tm, tn), lambda j, i: (i, j)),
        scratch_shapes=[pltpu.VMEM((GLA_V_WIDTH, tn), BF16), pltpu.VMEM((D_MODEL, tn), BF16)],
        compiler_params=_params(("parallel", "arbitrary")),
        name="branch",
    )(o_gla, o_diff, w_branch, w_branch, u, u)


def _outproj_kernel(mixed_ref, x_ref, w_ref, g_ref, x1_ref, h2_ref):
    x1 = x_ref[...] + _dot(mixed_ref[...], w_ref[...])
    x1_ref[...] = x1
    h2_ref[...] = (_rms_scale(x1, EPS) * g_ref[...]).astype(BF16)


def _out_proj(mixed, x2d, w_out, g2, *, tm):
    m = x2d.shape[0]
    return pl.pallas_call(
        _outproj_kernel,
        out_shape=(jax.ShapeDtypeStruct((m, D_MODEL), F32),
                   jax.ShapeDtypeStruct((m, D_MODEL), BF16)),
        grid=(m // tm,),
        in_specs=[
            pl.BlockSpec((tm, D_MODEL), lambda i: (i, 0)),
            pl.BlockSpec((tm, D_MODEL), lambda i: (i, 0)),
            pl.BlockSpec((D_MODEL, D_MODEL), lambda i: (0, 0)),
            pl.BlockSpec((1, D_MODEL), lambda i: (0, 0)),
        ],
        out_specs=(pl.BlockSpec((tm, D_MODEL), lambda i: (i, 0)),
                   pl.BlockSpec((tm, D_MODEL), lambda i: (i, 0))),
        compiler_params=_params(("parallel",)),
        name="out_proj",
    )(mixed, x2d, w_out, g2)


def _ffn_up_kernel(h2_ref, wa_ref, wv_ref, cw_ref, cb_ref, act_ref, wa_scr, wv_scr, carry_scr, *,
                   tiles_per_seq):
    i = pl.program_id(1)

    @pl.when(i == 0)
    def _():
        wa_scr[...] = wa_ref[...].astype(BF16)
        wv_scr[...] = wv_ref[...].astype(BF16)

    @pl.when(i % tiles_per_seq == 0)
    def _():
        carry_scr[...] = jnp.zeros_like(carry_scr)

    h2 = h2_ref[...]
    a = _dot(h2, wa_scr[...])
    val = _dot(h2, wv_scr[...])
    tm = a.shape[0]
    carry = carry_scr[...]
    row = lax.broadcasted_iota(jnp.int32, carry.shape, 0)

    def shifted(k):
        r = pltpu.roll(a, k, 0)
        head = jnp.where(row < k, pltpu.roll(carry, k, 0), r[:SUBLANES])
        return jnp.concatenate([head, r[SUBLANES:]], axis=0)

    conv = cb_ref[...] + shifted(2) * cw_ref[0:1, :] + shifted(1) * cw_ref[1:2, :] + a * cw_ref[2:3, :]
    carry_scr[...] = a[tm - SUBLANES:]
    act_ref[...] = (conv * _sigmoid(conv) * val).astype(BF16)


def _ffn_up(h2, w_up, conv_w, conv_b, *, seq, tm, tf):
    m = h2.shape[0]
    nf = FFN_HIDDEN // tf
    kernel = functools.partial(_ffn_up_kernel, tiles_per_seq=seq // tm)
    return pl.pallas_call(
        kernel,
        out_shape=jax.ShapeDtypeStruct((m, FFN_HIDDEN), BF16),
        grid=(nf, m // tm),
        in_specs=[
            pl.BlockSpec((tm, D_MODEL), lambda f, i: (i, 0)),
            pl.BlockSpec((None, D_MODEL, tf), lambda f, i: (0, 0, f)),
            pl.BlockSpec((None, D_MODEL, tf), lambda f, i: (0, 0, nf + f)),
            pl.BlockSpec((None, CONV_WIDTH, tf), lambda f, i: (0, 0, f)),
            pl.BlockSpec((1, tf), lambda f, i: (0, f)),
        ],
        out_specs=pl.BlockSpec((tm, tf), lambda f, i: (i, f)),
        scratch_shapes=[pltpu.VMEM((D_MODEL, tf), BF16), pltpu.VMEM((D_MODEL, tf), BF16),
                        pltpu.VMEM((SUBLANES, tf), F32)],
        compiler_params=_params(("parallel", "arbitrary")),
        name="ffn_up",
    )(h2, w_up, w_up, conv_w, conv_b)


def _ffn_down_kernel(act_ref, w_ref, x1_ref, o_ref):
    @pl.when(pl.program_id(1) == 0)
    def _():
        o_ref[...] = x1_ref[...]

    o_ref[...] += _dot(act_ref[...], w_ref[...].astype(BF16))


def _ffn_down(act, w_down, x1, *, tm, tk):
    m = x1.shape[0]
    return pl.pallas_call(
        _ffn_down_kernel,
        out_shape=jax.ShapeDtypeStruct((m, D_MODEL), F32),
        grid=(m // tm, FFN_HIDDEN // tk),
        in_specs=[
            pl.BlockSpec((tm, tk), lambda i, k: (i, k)),
            pl.BlockSpec((None, tk, D_MODEL), lambda i, k: (0, k, 0)),
            pl.BlockSpec((tm, D_MODEL), lambda i, k: (i, 0)),
        ],
        out_specs=pl.BlockSpec((tm, D_MODEL), lambda i, k: (i, 0)),
        compiler_params=_params(("parallel", "arbitrary")),
        name="ffn_down",
    )(act, w_down, x1)


def kernel(x, positions, norm1_g, w_in, w_gk_up, b_gk, gla_norm_g, q_norm_g, k_norm_g, lambda_q1,
           lambda_k1, lambda_q2, lambda_k2, subln_g, w_branch, w_out, norm2_g, w_ffn_up, conv_w,
           conv_b, w_ffn_down):
    batch, seq, _ = x.shape
    m = batch * seq
    l = 0
    x2d = x.reshape(m, D_MODEL)
    pos2d = positions.reshape(m, 1)
    wup_pad = jnp.pad(w_gk_up[l], ((0, LANES - GLA_GATE_RANK), (0, 0)))
    qk_gain = jnp.stack([q_norm_g[l], k_norm_g[l]])[:, None, :]

    w_in_t = jnp.swapaxes(w_in, 1, 2)

    h, glr = _norm1(x2d, norm1_g[l][None, :], w_in_t, tm=512)
    cos_tab, sin_tab = _rope_tables(pos2d, tm=1024)
    u = _in_proj(h, w_in_t, cos_tab, sin_tab, qk_gain, tm=2048, tn=512, rc=256)
    o_gla = _gla(u, glr, wup_pad, b_gk[l][None, :], gla_norm_g[l][None, :], batch=batch, seq=seq, tb=512)
    o_diff = _diff_attn(u, lambda_q1[l][None, :], lambda_k1[l][None, :], lambda_q2[l][None, :],
                        lambda_k2[l][None, :], subln_g[l][None, :], batch=batch, seq=seq, blk=512, rc=64)
    mixed = _branch(o_gla, o_diff, w_branch, u, tm=1024, tn=512)
    x1, h2 = _out_proj(mixed, x2d, w_out[l].astype(BF16), norm2_g[l][None, :], tm=512)
    act = _ffn_up(h2, w_ffn_up, conv_w, conv_b[l][None, :], seq=seq, tm=1024, tf=512)
    out = _ffn_down(act, w_ffn_down, x1, tm=1024, tk=512)
    return out.reshape(batch, seq, D_MODEL)
```

```python
import functools
import math

import jax
import jax.numpy as jnp
from jax import lax
from jax.experimental import pallas as pl
from jax.experimental.pallas import tpu as pltpu

F32 = jnp.float32
BF16 = jnp.bfloat16

D_MODEL = 2048
GLA_HEADS = 4
GLA_DK = 256
GLA_DV = 512
GLA_K_WIDTH = GLA_HEADS * GLA_DK
GLA_V_WIDTH = GLA_HEADS * GLA_DV
GLA_GATE_RANK = 16
GLA_GATE_NORMALIZER = 16.0
GLA_CHUNK = 64
DIFF_HEAD_DIM = 128
DIFF_HEADS = 8
DIFF_V_DIM = 256
ROPE_THETA = 500000.0
ROPE_DIM = 32
FFN_HIDDEN = 5632
CONV_WIDTH = 3
EPS = 1e-6
SUBLN_EPS = 1e-5
LAMBDA_INIT = 0.8 - 0.6 * math.exp(-0.3 * 0)
LOG2_E = math.log2(math.e)

LR_START = 2 * GLA_K_WIDTH + GLA_V_WIDTH
LR_END = LR_START + GLA_GATE_RANK
PACKED_WIDTH = 16384
COL_GQ, COL_GK, COL_GV, COL_GOUT = 0, 1024, 2048, 4096
COL_DQ, COL_DK, COL_DV, COL_MGLA, COL_MDIFF = 6144, 8192, 10240, 12288, 14336

LANES = 128
SUBLANES = 8
MXU_DIM = 256
VMEM_LIMIT_BYTES = 56 * 1024 * 1024

NT_DIMS = (((1,), (1,)), ((), ()))
TN_DIMS = (((0,), (0,)), ((), ()))


def _params(semantics):
    return pltpu.CompilerParams(dimension_semantics=semantics, vmem_limit_bytes=VMEM_LIMIT_BYTES)


def _sigmoid(x):
    return 1.0 / (1.0 + jnp.exp(-x))


def _log_sigmoid(x):
    return jnp.minimum(x, 0.0) - jnp.log(1.0 + jnp.exp(-jnp.abs(x)))


def _rms_scale(x, eps):
    return x * lax.rsqrt(jnp.mean(x * x, axis=-1, keepdims=True) + eps)


def _split_bf16(x):
    hi = x.astype(BF16)
    lo = (x - hi.astype(F32)).astype(BF16)
    return hi, lo


def _dot(a, b):
    return jnp.dot(a, b, preferred_element_type=F32)


def _dot_nt(a, b):
    return lax.dot_general(a, b, NT_DIMS, preferred_element_type=F32)


def _norm1_kernel(x_ref, g_ref, wlr_ref, h_ref, glr_ref):
    h = (_rms_scale(x_ref[...], EPS) * g_ref[...]).astype(BF16)
    h_ref[...] = h
    glr_ref[...] = _dot_nt(h, wlr_ref[...].astype(BF16))


def _norm1(x2d, g, w_in_t, *, tm):
    m = x2d.shape[0]
    return pl.pallas_call(
        _norm1_kernel,
        out_shape=(jax.ShapeDtypeStruct((m, D_MODEL), BF16),
                   jax.ShapeDtypeStruct((m, LANES), F32)),
        grid=(m // tm,),
        in_specs=[
            pl.BlockSpec((tm, D_MODEL), lambda i: (i, 0)),
            pl.BlockSpec((1, D_MODEL), lambda i: (0, 0)),
            pl.BlockSpec((None, LANES, D_MODEL), lambda i: (0, LR_START // LANES, 0)),
        ],
        out_specs=(pl.BlockSpec((tm, D_MODEL), lambda i: (i, 0)),
                   pl.BlockSpec((tm, LANES), lambda i: (i, 0))),
        compiler_params=_params(("parallel",)),
        name="norm1",
    )(x2d, g, w_in_t)


def _rope_kernel(pos_ref, freq_ref, sgn_ref, cos_ref, sin_ref):
    ang = pos_ref[...].astype(F32) * freq_ref[...]
    cos_ref[...] = jnp.cos(ang)
    sin_ref[...] = jnp.sin(ang) * sgn_ref[...]


def _rope_tables(pos2d, *, tm):
    m = pos2d.shape[0]
    inv_freq = ROPE_THETA ** (-jnp.arange(0, ROPE_DIM, 2, dtype=F32) / ROPE_DIM)
    pad = jnp.zeros((DIFF_HEAD_DIM - ROPE_DIM,), F32)
    freq = jnp.concatenate([inv_freq, inv_freq, pad])[None, :]
    half = jnp.ones((ROPE_DIM // 2,), F32)
    sgn = jnp.concatenate([-half, half, pad])[None, :]
    vec = pl.BlockSpec((1, DIFF_HEAD_DIM), lambda i: (0, 0))
    tab = pl.BlockSpec((tm, DIFF_HEAD_DIM), lambda i: (i, 0))
    return pl.pallas_call(
        _rope_kernel,
        out_shape=(jax.ShapeDtypeStruct((m, DIFF_HEAD_DIM), F32),) * 2,
        grid=(m // tm,),
        in_specs=[pl.BlockSpec((tm, 1), lambda i: (i, 0)), vec, vec],
        out_specs=(tab, tab),
        compiler_params=_params(("parallel",)),
        name="rope_tab",
    )(pos2d, freq, sgn)


def _inproj_kernel(h_ref, wt_ref, cos_ref, sin_ref, qkg_ref, u_ref, *, tn, rc):
    col0 = pl.program_id(1) * tn
    is_q = (col0 >= COL_DQ) & (col0 < COL_DK)
    is_k = (col0 >= COL_DK) & (col0 < COL_DV)

    @pl.when(jnp.logical_not(is_q | is_k))
    def _():
        u_ref[...] = _dot_nt(h_ref[...], wt_ref[0].astype(BF16)).astype(BF16)

    @pl.when(is_q | is_k)
    def _():
        w = wt_ref[0].astype(BF16)
        scale = jnp.where(is_q, LOG2_E * DIFF_HEAD_DIM ** -0.5, 1.0).astype(F32)
        g = jnp.where(is_q, qkg_ref[0], qkg_ref[1])
        first_half = lax.broadcasted_iota(jnp.int32, (rc, DIFF_HEAD_DIM), 1) < ROPE_DIM // 2
        for c in range(h_ref.shape[0] // rc):
            rows = slice(c * rc, (c + 1) * rc)
            acc = _dot_nt(h_ref[rows, :], w)
            cosf = cos_ref[rows, :] * scale
            sinf = sin_ref[rows, :] * scale
            for hh in range(tn // DIFF_HEAD_DIM):
                cols = slice(hh * DIFF_HEAD_DIM, (hh + 1) * DIFF_HEAD_DIM)
                y = _rms_scale(acc[:, cols], EPS) * g
                partner = jnp.where(first_half,
                                    pltpu.roll(y, DIFF_HEAD_DIM - ROPE_DIM // 2, 1),
                                    pltpu.roll(y, ROPE_DIM // 2, 1))
                u_ref[rows, cols] = (y * cosf + partner * sinf).astype(BF16)


def _in_proj(h, w_in_t, cos_tab, sin_tab, qk_gain, *, tm, tn, rc):
    m = h.shape[0]
    kernel = functools.partial(_inproj_kernel, tn=tn, rc=rc)
    tab = pl.BlockSpec((tm, DIFF_HEAD_DIM), lambda i, j: (i, 0))

    def wt_map(i, j):
        row = j * tn + jnp.where(j * tn >= LR_START, GLA_GATE_RANK, 0)
        return (0, pl.multiple_of(row, GLA_GATE_RANK), 0)

    return pl.pallas_call(
        kernel,
        out_shape=jax.ShapeDtypeStruct((m, PACKED_WIDTH), BF16),
        grid=(m // tm, PACKED_WIDTH // tn),
        in_specs=[
            pl.BlockSpec((tm, D_MODEL), lambda i, j: (i, 0)),
            pl.BlockSpec((pl.Element(1), pl.Element(tn), pl.Element(D_MODEL)), wt_map),
            tab, tab,
            pl.BlockSpec((2, 1, DIFF_HEAD_DIM), lambda i, j: (0, 0, 0)),
        ],
        out_specs=pl.BlockSpec((tm, tn), lambda i, j: (i, j)),
        compiler_params=_params(("parallel", "arbitrary")),
        name="in_proj",
    )(h, w_in_t, cos_tab, sin_tab, qk_gain)


def _gla_kernel(q_ref, k_ref, v_ref, go_ref, glr_ref, wup_ref, bgk_ref, gn_ref, o_ref, s_scr, *, tb):
    @pl.when(pl.program_id(1) == 0)
    def _():
        s_scr[...] = jnp.zeros_like(s_scr)

    c = GLA_CHUNK
    gk = _dot(glr_ref[...].astype(BF16), wup_ref[...].astype(BF16)) + bgk_ref[...]
    log_decay = _log_sigmoid(gk) / GLA_GATE_NORMALIZER
    ld_hi, ld_lo = _split_bf16(log_decay)

    causal = lax.broadcasted_iota(jnp.int32, (c, c), 0) >= lax.broadcasted_iota(jnp.int32, (c, c), 1)
    tri = jnp.where(causal, 1.0, 0.0).astype(BF16)
    gn = gn_ref[...]
    for n in range(tb // c):
        rows = slice(n * c, (n + 1) * c)
        b = _dot(tri, ld_hi[rows]) + _dot(tri, ld_lo[rows])
        b_last = b[c - 1:c]
        q = q_ref[rows, :].astype(F32) * (GLA_DK ** -0.5)
        k = k_ref[rows, :].astype(F32)
        q_dec = (q * jnp.exp(b)).astype(BF16)
        k_inv = (k * jnp.exp(-b)).astype(BF16)
        k_tail = (k * jnp.exp(b_last - b)).astype(BF16)
        decay = jnp.exp(b_last)
        for h in range(GLA_HEADS):
            kc = slice(h * GLA_DK, (h + 1) * GLA_DK)
            vc = slice(h * GLA_DV, (h + 1) * GLA_DV)
            v = v_ref[rows, vc]
            a = jnp.where(causal, _dot_nt(q_dec[:, kc], k_inv[:, kc]), 0.0).astype(BF16)
            s_in = s_scr[h]
            o = _dot(a, v) + _dot_nt(q_dec[:, kc], s_in.astype(BF16))
            kv_t = lax.dot_general(v, k_tail[:, kc], TN_DIMS, preferred_element_type=F32)
            s_scr[h] = s_in * decay[:, kc] + kv_t
            go = go_ref[rows, vc].astype(F32)
            o_ref[rows, vc] = (_rms_scale(o, EPS) * gn * (go * _sigmoid(go))).astype(BF16)


def _gla(u, glr, wup_pad, b_gk, gla_norm_g, *, batch, seq, tb):
    m = u.shape[0]
    nt = seq // tb
    kernel = functools.partial(_gla_kernel, tb=tb)
    return pl.pallas_call(
        kernel,
        out_shape=jax.ShapeDtypeStruct((m, GLA_V_WIDTH), BF16),
        grid=(batch, nt),
        in_specs=[
            pl.BlockSpec((tb, GLA_K_WIDTH), lambda b, t: (b * nt + t, COL_GQ // GLA_K_WIDTH)),
            pl.BlockSpec((tb, GLA_K_WIDTH), lambda b, t: (b * nt + t, COL_GK // GLA_K_WIDTH)),
            pl.BlockSpec((tb, GLA_V_WIDTH), lambda b, t: (b * nt + t, COL_GV // GLA_V_WIDTH)),
            pl.BlockSpec((tb, GLA_V_WIDTH), lambda b, t: (b * nt + t, COL_GOUT // GLA_V_WIDTH)),
            pl.BlockSpec((tb, LANES), lambda b, t: (b * nt + t, 0)),
            pl.BlockSpec((LANES, GLA_K_WIDTH), lambda b, t: (0, 0)),
            pl.BlockSpec((1, GLA_K_WIDTH), lambda b, t: (0, 0)),
            pl.BlockSpec((1, GLA_DV), lambda b, t: (0, 0)),
        ],
        out_specs=pl.BlockSpec((tb, GLA_V_WIDTH), lambda b, t: (b * nt + t, 0)),
        scratch_shapes=[pltpu.VMEM((GLA_HEADS, GLA_DV, GLA_DK), F32)],
        compiler_params=_params(("parallel", "arbitrary")),
        name="gla",
    )(u, u, u, u, glr, wup_pad, b_gk, gla_norm_g)


def _attn_kernel(lq1_ref, lk1_ref, lq2_ref, lk2_ref, sg_ref, q_ref, k_ref, v_ref, o_ref,
                 kt_scr, m_scr, l_scr, a_scr, acc_scr, s_scr, p_scr, *, blk, rc):
    qi = pl.program_id(2)
    d = DIFF_HEAD_DIM
    q = q_ref[...]
    qs = (q[:, :d], q[:, d:])

    @pl.when(qi == 0)
    def _():
        for j in range(k_ref.shape[0] // blk):
            kt_scr[j] = k_ref[j * blk:(j + 1) * blk, :].T

    m_scr[...] = jnp.full_like(m_scr, -jnp.inf)
    l_scr[...] = jnp.zeros_like(l_scr)
    acc_scr[...] = jnp.zeros_like(acc_scr)

    def scores(j, buf):
        for sub in range(2):
            for n in range(blk // MXU_DIM):
                ksub = kt_scr[j, sub * d:(sub + 1) * d, n * MXU_DIM:(n + 1) * MXU_DIM]
                s_scr[buf, sub, :, n * MXU_DIM:(n + 1) * MXU_DIM] = _dot(qs[sub], ksub)

    def softmax_pv(j, buf, masked):
        start = pl.multiple_of(j * blk, blk)
        vblk = v_ref[pl.ds(start, blk), :]
        for sub in range(2):
            sub_rows = slice(sub * blk, (sub + 1) * blk)

            def score_cols(rows, row_base):
                cols = [s_scr[buf, sub, rows, c * LANES:(c + 1) * LANES] for c in range(blk // LANES)]
                if masked:
                    shape = cols[0].shape
                    ri = row_base + lax.broadcasted_iota(jnp.int32, shape, 0)
                    ci = lax.broadcasted_iota(jnp.int32, shape, 1)
                    cols = [jnp.where(ri >= ci + c * LANES, col, -jnp.inf) for c, col in enumerate(cols)]
                return cols

            m_old = m_scr[sub_rows, :]
            row_max = jnp.max(functools.reduce(jnp.maximum, score_cols(slice(None), 0)),
                              axis=-1, keepdims=True)
            m_new = jnp.maximum(m_old, row_max)
            a_scr[buf, sub_rows, :] = jnp.exp2(m_old - m_new)
            m_scr[sub_rows, :] = m_new
            for r in range(blk // rc):
                rows = slice(r * rc, (r + 1) * rc)
                srows = slice(sub * blk + r * rc, sub * blk + (r + 1) * rc)
                m_rows = m_scr[srows, :]
                ps = [jnp.exp2(col - m_rows) for col in score_cols(rows, r * rc)]
                l_scr[srows, :] = a_scr[buf, srows, :] * l_scr[srows, :] + functools.reduce(jnp.add, ps)
                p_scr[buf, srows, :] = jnp.concatenate(ps, axis=1).astype(BF16)
        pv = functools.reduce(jnp.add, [
            _dot(p_scr[buf, :, n * MXU_DIM:(n + 1) * MXU_DIM], vblk[n * MXU_DIM:(n + 1) * MXU_DIM, :])
            for n in range(blk // MXU_DIM)])
        alpha = a_scr[buf]
        acc_scr[...] = jnp.concatenate([alpha, alpha], axis=1) * acc_scr[...] + pv

    def pair(i, carry):
        j = 2 * i
        scores(j + 1, 1)
        softmax_pv(j, 0, masked=False)
        scores(j + 2, 0)
        softmax_pv(j + 1, 1, masked=False)
        return carry

    scores(0, 0)
    lax.fori_loop(0, qi // 2, pair, 0)

    @pl.when(qi % 2 == 1)
    def _():
        scores(qi, 1)
        softmax_pv(qi - 1, 0, masked=False)
        softmax_pv(qi, 1, masked=True)

    @pl.when(qi % 2 == 0)
    def _():
        softmax_pv(qi, 0, masked=True)

    lam = (jnp.exp(jnp.sum(lq1_ref[...] * lk1_ref[...], axis=-1, keepdims=True))
           - jnp.exp(jnp.sum(lq2_ref[...] * lk2_ref[...], axis=-1, keepdims=True))
           + LAMBDA_INIT)
    l1 = jnp.sum(l_scr[:blk, :], axis=-1, keepdims=True)
    l2 = jnp.sum(l_scr[blk:, :], axis=-1, keepdims=True)
    o = acc_scr[:blk, :] / l1 - lam * (acc_scr[blk:, :] / l2)
    o = _rms_scale(o, SUBLN_EPS) * sg_ref[...] * (1.0 - LAMBDA_INIT)
    o_ref[...] = o.astype(BF16)


def _diff_attn(u, lq1, lk1, lq2, lk2, subln_g, *, batch, seq, blk, rc):
    m = u.shape[0]
    tq = blk
    nq = seq // tq
    width = 2 * DIFF_HEAD_DIM
    vec = pl.BlockSpec((1, DIFF_HEAD_DIM), lambda b, h, i: (0, 0))
    kernel = functools.partial(_attn_kernel, blk=blk, rc=rc)
    return pl.pallas_call(
        kernel,
        out_shape=jax.ShapeDtypeStruct((m, DIFF_HEADS * DIFF_V_DIM), BF16),
        grid=(batch, DIFF_HEADS, nq),
        in_specs=[
            vec, vec, vec, vec,
            pl.BlockSpec((1, DIFF_V_DIM), lambda b, h, i: (0, 0)),
            pl.BlockSpec((tq, width), lambda b, h, i: (b * nq + i, COL_DQ // width + h)),
            pl.BlockSpec((seq, width), lambda b, h, i: (b, COL_DK // width + h)),
            pl.BlockSpec((seq, DIFF_V_DIM), lambda b, h, i: (b, COL_DV // DIFF_V_DIM + h)),
        ],
        out_specs=pl.BlockSpec((tq, DIFF_V_DIM), lambda b, h, i: (b * nq + i, h)),
        scratch_shapes=[pltpu.VMEM((seq // blk, width, blk), BF16),
                        pltpu.VMEM((2 * blk, LANES), F32),
                        pltpu.VMEM((2 * blk, LANES), F32),
                        pltpu.VMEM((2, 2 * blk, LANES), F32),
                        pltpu.VMEM((2 * blk, DIFF_V_DIM), F32),
                        pltpu.VMEM((2, 2, blk, blk), F32),
                        pltpu.VMEM((2, 2 * blk, blk), BF16)],
        compiler_params=_params(("parallel", "parallel", "arbitrary")),
        name="diff_attn",
    )(lq1, lk1, lq2, lk2, subln_g, u, u, u)


def _branch_kernel(og_ref, od_ref, wg_ref, wd_ref, mg_ref, md_ref, o_ref, wg_scr, wd_scr):
    @pl.when(pl.program_id(1) == 0)
    def _():
        wg_scr[...] = wg_ref[...].astype(BF16)
        wd_scr[...] = wd_ref[...].astype(BF16)

    yg = _dot(og_ref[...], wg_scr[...])
    yd = _dot(od_ref[...], wd_scr[...])
    mixed = _sigmoid(mg_ref[...].astype(F32)) * yg + _sigmoid(md_ref[...].astype(F32)) * yd
    o_ref[...] = mixed.astype(BF16)


def _branch(o_gla, o_diff, w_branch, u, *, tm, tn):
    m = u.shape[0]
    nrow = GLA_V_WIDTH // D_MODEL
    return pl.pallas_call(
        _branch_kernel,
        out_shape=jax.ShapeDtypeStruct((m, D_MODEL), BF16),
        grid=(D_MODEL // tn, m // tm),
        in_specs=[
            pl.BlockSpec((tm, GLA_V_WIDTH), lambda j, i: (i, 0)),
            pl.BlockSpec((tm, D_MODEL), lambda j, i: (i, 0)),
            pl.BlockSpec((None, GLA_V_WIDTH, tn), lambda j, i: (0, 0, j)),
            pl.BlockSpec((None, D_MODEL, tn), lambda j, i: (0, nrow, j)),
            pl.BlockSpec((tm, tn), lambda j, i: (i, COL_MGLA // tn + j)),
            pl.BlockSpec((tm, tn), lambda j, i: (i, COL_MDIFF // tn + j)),
        ],
        out_specs=pl.BlockSpec((tm, tn), lambda j, i: (i, j)),
        scratch_shapes=[pltpu.VMEM((GLA_V_WIDTH, tn), BF16), pltpu.VMEM((D_MODEL, tn), BF16)],
        compiler_params=_params(("parallel", "arbitrary")),
        name="branch",
    )(o_gla, o_diff, w_branch, w_branch, u, u)


def _outproj_kernel(mixed_ref, x_ref, w_ref, g_ref, x1_ref, h2_ref):
    x1 = x_ref[...] + _dot(mixed_ref[...], w_ref[...])
    x1_ref[...] = x1
    h2_ref[...] = (_rms_scale(x1, EPS) * g_ref[...]).astype(BF16)


def _out_proj(mixed, x2d, w_out, g2, *, tm):
    m = x2d.shape[0]
    return pl.pallas_call(
        _outproj_kernel,
        out_shape=(jax.ShapeDtypeStruct((m, D_MODEL), F32),
                   jax.ShapeDtypeStruct((m, D_MODEL), BF16)),
        grid=(m // tm,),
        in_specs=[
            pl.BlockSpec((tm, D_MODEL), lambda i: (i, 0)),
            pl.BlockSpec((tm, D_MODEL), lambda i: (i, 0)),
            pl.BlockSpec((D_MODEL, D_MODEL), lambda i: (0, 0)),
            pl.BlockSpec((1, D_MODEL), lambda i: (0, 0)),
        ],
        out_specs=(pl.BlockSpec((tm, D_MODEL), lambda i: (i, 0)),
                   pl.BlockSpec((tm, D_MODEL), lambda i: (i, 0))),
        compiler_params=_params(("parallel",)),
        name="out_proj",
    )(mixed, x2d, w_out, g2)


def _ffn_up_kernel(h2_ref, wa_ref, wv_ref, cw_ref, cb_ref, act_ref, wa_scr, wv_scr, carry_scr, *,
                   tiles_per_seq):
    i = pl.program_id(1)

    @pl.when(i == 0)
    def _():
        wa_scr[...] = wa_ref[...].astype(BF16)
        wv_scr[...] = wv_ref[...].astype(BF16)

    @pl.when(i % tiles_per_seq == 0)
    def _():
        carry_scr[...] = jnp.zeros_like(carry_scr)

    h2 = h2_ref[...]
    a = _dot(h2, wa_scr[...])
    val = _dot(h2, wv_scr[...])
    tm = a.shape[0]
    carry = carry_scr[...]
    row = lax.broadcasted_iota(jnp.int32, carry.shape, 0)

    def shifted(k):
        r = pltpu.roll(a, k, 0)
        head = jnp.where(row < k, pltpu.roll(carry, k, 0), r[:SUBLANES])
        return jnp.concatenate([head, r[SUBLANES:]], axis=0)

    conv = cb_ref[...] + shifted(2) * cw_ref[0:1, :] + shifted(1) * cw_ref[1:2, :] + a * cw_ref[2:3, :]
    carry_scr[...] = a[tm - SUBLANES:]
    act_ref[...] = (conv * _sigmoid(conv) * val).astype(BF16)


def _ffn_up(h2, w_up, conv_w, conv_b, *, seq, tm, tf):
    m = h2.shape[0]
    nf = FFN_HIDDEN // tf
    kernel = functools.partial(_ffn_up_kernel, tiles_per_seq=seq // tm)
    return pl.pallas_call(
        kernel,
        out_shape=jax.ShapeDtypeStruct((m, FFN_HIDDEN), BF16),
        grid=(nf, m // tm),
        in_specs=[
            pl.BlockSpec((tm, D_MODEL), lambda f, i: (i, 0)),
            pl.BlockSpec((None, D_MODEL, tf), lambda f, i: (0, 0, f)),
            pl.BlockSpec((None, D_MODEL, tf), lambda f, i: (0, 0, nf + f)),
            pl.BlockSpec((None, CONV_WIDTH, tf), lambda f, i: (0, 0, f)),
            pl.BlockSpec((1, tf), lambda f, i: (0, f)),
        ],
        out_specs=pl.BlockSpec((tm, tf), lambda f, i: (i, f)),
        scratch_shapes=[pltpu.VMEM((D_MODEL, tf), BF16), pltpu.VMEM((D_MODEL, tf), BF16),
                        pltpu.VMEM((SUBLANES, tf), F32)],
        compiler_params=_params(("parallel", "arbitrary")),
        name="ffn_up",
    )(h2, w_up, w_up, conv_w, conv_b)


def _ffn_down_kernel(act_ref, w_ref, x1_ref, o_ref):
    @pl.when(pl.program_id(1) == 0)
    def _():
        o_ref[...] = x1_ref[...]

    o_ref[...] += _dot(act_ref[...], w_ref[...].astype(BF16))


def _ffn_down(act, w_down, x1, *, tm, tk):
    m = x1.shape[0]
    return pl.pallas_call(
        _ffn_down_kernel,
        out_shape=jax.ShapeDtypeStruct((m, D_MODEL), F32),
        grid=(m // tm, FFN_HIDDEN // tk),
        in_specs=[
            pl.BlockSpec((tm, tk), lambda i, k: (i, k)),
            pl.BlockSpec((None, tk, D_MODEL), lambda i, k: (0, k, 0)),
            pl.BlockSpec((tm, D_MODEL), lambda i, k: (i, 0)),
        ],
        out_specs=pl.BlockSpec((tm, D_MODEL), lambda i, k: (i, 0)),
        compiler_params=_params(("parallel", "arbitrary")),
        name="ffn_down",
    )(act, w_down, x1)


def kernel(x, positions, norm1_g, w_in, w_gk_up, b_gk, gla_norm_g, q_norm_g, k_norm_g, lambda_q1,
           lambda_k1, lambda_q2, lambda_k2, subln_g, w_branch, w_out, norm2_g, w_ffn_up, conv_w,
           conv_b, w_ffn_down):
    batch, seq, _ = x.shape
    m = batch * seq
    l = 0
    x2d = x.reshape(m, D_MODEL)
    pos2d = positions.reshape(m, 1)
    wup_pad = jnp.pad(w_gk_up[l], ((0, LANES - GLA_GATE_RANK), (0, 0)))
    qk_gain = jnp.stack([q_norm_g[l], k_norm_g[l]])[:, None, :]

    w_in_t = jnp.swapaxes(w_in, 1, 2)

    h, glr = _norm1(x2d, norm1_g[l][None, :], w_in_t, tm=512)
    cos_tab, sin_tab = _rope_tables(pos2d, tm=1024)
    u = _in_proj(h, w_in_t, cos_tab, sin_tab, qk_gain, tm=2048, tn=1024, rc=256)
    o_gla = _gla(u, glr, wup_pad, b_gk[l][None, :], gla_norm_g[l][None, :], batch=batch, seq=seq, tb=512)
    o_diff = _diff_attn(u, lambda_q1[l][None, :], lambda_k1[l][None, :], lambda_q2[l][None, :],
                        lambda_k2[l][None, :], subln_g[l][None, :], batch=batch, seq=seq, blk=512, rc=64)
    mixed = _branch(o_gla, o_diff, w_branch, u, tm=1024, tn=512)
    x1, h2 = _out_proj(mixed, x2d, w_out[l].astype(BF16), norm2_g[l][None, :], tm=512)
    act = _ffn_up(h2, w_ffn_up, conv_w, conv_b[l][None, :], seq=seq, tm=1024, tf=512)
    out = _ffn_down(act, w_ffn_down, x1, tm=1024, tk=512)
    return out.reshape(batch, seq, D_MODEL)
```
